```python
import jax, jax.numpy as jnp
from jax import lax
import numpy as np

D_MODEL = 1024
BATCH = 16
SEQ = 4096
DEPTH = 2
DEC_BATCH = 16
DEC_SEQ = 32
PAST_LEN = 1024

CHUNK = 64
HEAD_DIM = 64
H_MLSTM = 4
H_FOX = 8
G_GMLP = 4
D_MLSTM = H_MLSTM * HEAD_DIM
D_FOX = H_FOX * HEAD_DIM
D_GMLP = G_GMLP * HEAD_DIM
D_MIX = D_MLSTM + D_FOX + D_GMLP
MLP_CHUNK = 128
Q_BLOCK = 128
D_FF = ((8 * D_MODEL + 3 * 256 - 1) // (3 * 256)) * 256
IN_SIZES = (D_MLSTM, D_MLSTM, D_MLSTM, D_MLSTM, H_MLSTM, H_MLSTM, D_FOX, D_FOX, D_FOX, H_FOX, D_GMLP, D_GMLP)
N_IN = 4 * D_MLSTM + 2 * H_MLSTM + 3 * D_FOX + H_FOX + 2 * D_GMLP
ALPHA = (2 * DEPTH) ** 0.25
BETA = (8 * DEPTH) ** -0.25
LN_EPS = 1e-5
HN_EPS = 1e-6
FOX_SCALE = HEAD_DIM ** -0.5

kernel_name = "hybrid_mlstm_fox_gmlp_stream_step"


def _layer_norm(x, g, b):
    xf = x.astype(jnp.float32)
    mu = xf.mean(-1, keepdims=True)
    var = jnp.square(xf - mu).mean(-1, keepdims=True)
    return ((xf - mu) * lax.rsqrt(var + LN_EPS) * g + b).astype(x.dtype)


def _head_norm(h):
    mu = h.mean(-1, keepdims=True)
    var = jnp.square(h - mu).mean(-1, keepdims=True)
    return (h - mu) * lax.rsqrt(var + HN_EPS)


def _mlstm_chunk(carry, inp):
    C0, n0, m0 = carry
    q, k, v, li, lf = inp
    L = q.shape[1]
    bT = jnp.cumsum(lf, axis=1).transpose(0, 2, 1)
    liT = li.transpose(0, 2, 1)
    causal = jnp.tril(jnp.ones((L, L), bool))
    dmat = jnp.where(causal, bT[..., :, None] - bT[..., None, :] + liT[..., None, :], -jnp.inf)
    inter = m0[..., None] + bT
    m = jnp.maximum(inter, dmat.max(-1))
    w_intra = jnp.exp(dmat - m[..., None])
    w_inter = jnp.exp(inter - m)
    s = jnp.einsum('bthd,bshd->bhts', q, k) * w_intra
    num = (jnp.einsum('bhts,bshd->bthd', s, v)
           + jnp.einsum('bthk,bhkv->bthv', q, C0) * w_inter.transpose(0, 2, 1)[..., None])
    den = s.sum(-1) + jnp.einsum('bthk,bhk->bht', q, n0) * w_inter
    den = jnp.maximum(jnp.abs(den), jnp.exp(-m))
    h = num / den.transpose(0, 2, 1)[..., None]
    m_end = m[..., -1]
    w_end = jnp.exp(dmat[..., -1, :] - m_end[..., None])
    d0 = jnp.exp(inter[..., -1] - m_end)
    C1 = d0[..., None, None] * C0 + jnp.einsum('bhs,bshk,bshv->bhkv', w_end, k, v)
    n1 = d0[..., None] * n0 + jnp.einsum('bhs,bshk->bhk', w_end, k)
    return (C1, n1, m_end), h


def _mlstm(q, k, v, li, lf, state0, blk):
    B, S = q.shape[:2]
    nc = S // blk

    def to_blocks(a):
        return a.reshape(B, nc, blk, *a.shape[2:]).swapaxes(0, 1)

    final, h = lax.scan(_mlstm_chunk, state0,
                        (to_blocks(q), to_blocks(k), to_blocks(v), to_blocks(li), to_blocks(lf)))
    return h.swapaxes(0, 1).reshape(B, S, H_MLSTM, HEAD_DIM), final


def _fox_prompt(q, k, v, logf):
    B, S = q.shape[:2]
    nb = S // Q_BLOCK
    Ft = jnp.cumsum(logf, axis=1).transpose(0, 2, 1)
    qb = q.reshape(B, nb, Q_BLOCK, H_FOX, HEAD_DIM).swapaxes(0, 1)
    Fb = Ft.reshape(B, H_FOX, nb, Q_BLOCK).transpose(2, 0, 1, 3)
    pos_k = jnp.arange(S)

    def block(args):
        i, qi, Fi = args
        pos_q = i * Q_BLOCK + jnp.arange(Q_BLOCK)
        logits = (jnp.einsum('bthd,bshd->bhts', qi, k) * FOX_SCALE
                  + Fi[..., None] - Ft[:, :, None, :])
        logits = jnp.where(pos_k[None, :] <= pos_q[:, None], logits, -jnp.inf)
        p = jax.nn.softmax(logits, axis=-1)
        return jnp.einsum('bhts,bshd->bthd', p, v)

    out = lax.map(block, (jnp.arange(nb), qb, Fb))
    return out.swapaxes(0, 1).reshape(B, S, H_FOX, HEAD_DIM)


def _fox_sample(q, k, v, logf, k_c, v_c, logf_c):
    P, T = k_c.shape[1], q.shape[1]
    k_all = jnp.concatenate([k_c, k], axis=1)
    v_all = jnp.concatenate([v_c, v], axis=1)
    Ft = jnp.cumsum(jnp.concatenate([logf_c, logf], axis=1), axis=1).transpose(0, 2, 1)
    logits = (jnp.einsum('bthd,bshd->bhts', q, k_all) * FOX_SCALE
              + Ft[:, :, P:, None] - Ft[:, :, None, :])
    mask = jnp.arange(P + T)[None, :] <= (P + jnp.arange(T))[:, None]
    p = jax.nn.softmax(jnp.where(mask, logits, -jnp.inf), axis=-1)
    return jnp.einsum('bhts,bshd->bthd', p, v_all)


def _spatial_gate(u, vn, ws, bs):
    B, S, _ = u.shape
    L = min(S, MLP_CHUNK)
    w = jnp.where(jnp.tril(jnp.ones((L, L), bool)), ws[:, :L, :L], 0.0)
    vc = vn.reshape(B, S // L, L, G_GMLP, HEAD_DIM)
    z = jnp.einsum('gts,bnsgc->bntgc', w, vc) + bs[:, :L].T[None, None, :, :, None]
    return u * z.reshape(B, S, D_GMLP).astype(u.dtype)


def _layer(x, c, p, cache):
    (w_ada, b_ada, w_in, b_mlstm_i, b_mlstm_f, mlstm_norm_g, b_fox_f, gmlp_ln_g, gmlp_ln_b,
     gmlp_ws, gmlp_bs, w_o, ln1_g, ln1_b, w_gate, w_up, w_down, ln2_g, ln2_b) = p
    B, S, _ = x.shape
    dt = x.dtype
    f32 = jnp.float32
    mod = jax.nn.silu(c) @ w_ada + b_ada
    sh1, sc1, g1, sh2, sc2, g2 = jnp.split(mod[:, None, :], 6, axis=-1)
    h = x * (1 + sc1) + sh1
    split_at = np.cumsum(IN_SIZES)[:-1].tolist()
    mq, mk, mv, mo, mi, mf, fq, fk, fv, ff, gu, gv = jnp.split(h @ w_in, split_at, axis=-1)

    def heads(a, n):
        return a.reshape(B, S, n, HEAD_DIM).astype(f32)

    li = mi.astype(f32) + b_mlstm_i
    lf = jax.nn.log_sigmoid(mf.astype(f32) + b_mlstm_f)
    if cache is None:
        st0 = (jnp.zeros((B, H_MLSTM, HEAD_DIM, HEAD_DIM), f32),
               jnp.zeros((B, H_MLSTM, HEAD_DIM), f32),
               jnp.full((B, H_MLSTM), -jnp.inf, f32))
        blk = CHUNK
    else:
        st0 = (cache[3].astype(f32), cache[4].astype(f32), cache[5].astype(f32))
        blk = S
    hm, (C1, n1, m1) = _mlstm(heads(mq, H_MLSTM), heads(mk, H_MLSTM) * HEAD_DIM ** -0.5,
                              heads(mv, H_MLSTM), li, lf, st0, blk)
    hm = (_head_norm(hm) * mlstm_norm_g.reshape(H_MLSTM, HEAD_DIM)
          * jax.nn.sigmoid(heads(mo, H_MLSTM)))
    fk_h, fv_h = heads(fk, H_FOX), heads(fv, H_FOX)
    logf = jax.nn.log_sigmoid(ff.astype(f32) + b_fox_f)
    if cache is None:
        hf = _fox_prompt(heads(fq, H_FOX), fk_h, fv_h, logf)
    else:
        hf = _fox_sample(heads(fq, H_FOX), fk_h, fv_h, logf,
                         cache[0].astype(f32), cache[1].astype(f32), cache[2].astype(f32))
    vn = _layer_norm(gv, gmlp_ln_g, gmlp_ln_b)
    hg = _spatial_gate(gu, vn, gmlp_ws, gmlp_bs)
    mix = jnp.concatenate([hm.reshape(B, S, D_MLSTM).astype(dt),
                           hf.reshape(B, S, D_FOX).astype(dt), hg], axis=-1) @ w_o
    x = _layer_norm(ALPHA * x + (1 + g1) * mix, ln1_g, ln1_b)
    h = x * (1 + sc2) + sh2
    y = (jax.nn.silu(h @ w_gate) * (h @ w_up)) @ w_down
    x = _layer_norm(ALPHA * x + (1 + g2) * y, ln2_g, ln2_b)
    new = (fk_h.astype(dt), fv_h.astype(dt), logf.astype(dt),
           C1.astype(dt), n1.astype(dt), m1.astype(dt), vn)
    return x, new


def setup_inputs(seed: int = 0) -> dict:
    key = jax.random.key(seed)
    ks = jax.random.split(key, 40)
    cnt = [0]

    def nk():
        cnt[0] += 1
        return ks[cnt[0] - 1]

    def nrm(shape, s=1.0):
        return s * jax.random.normal(nk(), shape, jnp.float32)

    def uni(shape, lo, hi):
        return jax.random.uniform(nk(), shape, jnp.float32, lo, hi)

    return {
        'x_prompt': nrm((BATCH, SEQ, D_MODEL)),
        'x_sample': nrm((DEC_BATCH, DEC_SEQ, D_MODEL)),
        'cache_fox_k': nrm((DEPTH, DEC_BATCH, PAST_LEN, H_FOX, HEAD_DIM)),
        'cache_fox_v': nrm((DEPTH, DEC_BATCH, PAST_LEN, H_FOX, HEAD_DIM)),
        'cache_fox_logf': jax.nn.log_sigmoid(3.5 + nrm((DEPTH, DEC_BATCH, PAST_LEN, H_FOX))),
        'state_mlstm_C': nrm((DEPTH, DEC_BATCH, H_MLSTM, HEAD_DIM, HEAD_DIM), 0.5),
        'state_mlstm_n': nrm((DEPTH, DEC_BATCH, H_MLSTM, HEAD_DIM), 0.5),
        'state_mlstm_m': nrm((DEPTH, DEC_BATCH, H_MLSTM)),
        'c_prompt': nrm((BATCH, D_MODEL)),
        'c_sample': nrm((DEC_BATCH, D_MODEL)),
        'w_ada': nrm((DEPTH, D_MODEL, 6 * D_MODEL), 0.5 * D_MODEL ** -0.5),
        'b_ada': nrm((DEPTH, 6 * D_MODEL), 0.05),
        'w_in': nrm((DEPTH, D_MODEL, N_IN), D_MODEL ** -0.5),
        'b_mlstm_i': nrm((DEPTH, H_MLSTM), 0.1),
        'b_mlstm_f': uni((DEPTH, H_MLSTM), 3.0, 6.0),
        'mlstm_norm_g': 1.0 + nrm((DEPTH, D_MLSTM), 0.05),
        'b_fox_f': uni((DEPTH, H_FOX), 2.0, 5.0),
        'gmlp_ln_g': 1.0 + nrm((DEPTH, D_GMLP), 0.05),
        'gmlp_ln_b': nrm((DEPTH, D_GMLP), 0.05),
        'gmlp_ws': nrm((DEPTH, G_GMLP, MLP_CHUNK, MLP_CHUNK), MLP_CHUNK ** -0.5),
        'gmlp_bs': 1.0 + nrm((DEPTH, G_GMLP, MLP_CHUNK), 0.1),
        'w_o': nrm((DEPTH, D_MIX, D_MODEL), BETA * D_MIX ** -0.5),
        'ln1_g': 1.0 + nrm((DEPTH, D_MODEL), 0.05),
        'ln1_b': nrm((DEPTH, D_MODEL), 0.05),
        'w_gate': nrm((DEPTH, D_MODEL, D_FF), D_MODEL ** -0.5),
        'w_up': nrm((DEPTH, D_MODEL, D_FF), D_MODEL ** -0.5),
        'w_down': nrm((DEPTH, D_FF, D_MODEL), BETA * D_FF ** -0.5),
        'ln2_g': 1.0 + nrm((DEPTH, D_MODEL), 0.05),
        'ln2_b': nrm((DEPTH, D_MODEL), 0.05),
    }


def reference(x_prompt, x_sample, cache_fox_k, cache_fox_v, cache_fox_logf, state_mlstm_C, state_mlstm_n,
              state_mlstm_m, c_prompt, c_sample, w_ada, b_ada, w_in, b_mlstm_i, b_mlstm_f, mlstm_norm_g,
              b_fox_f, gmlp_ln_g, gmlp_ln_b, gmlp_ws, gmlp_bs, w_o, ln1_g, ln1_b, w_gate, w_up, w_down,
              ln2_g, ln2_b):
    xp, xs = x_prompt, x_sample
    new_p, new_s = [], []
    for l in range(DEPTH):
        p = (w_ada[l], b_ada[l], w_in[l], b_mlstm_i[l], b_mlstm_f[l], mlstm_norm_g[l], b_fox_f[l],
             gmlp_ln_g[l], gmlp_ln_b[l], gmlp_ws[l], gmlp_bs[l], w_o[l], ln1_g[l], ln1_b[l],
             w_gate[l], w_up[l], w_down[l], ln2_g[l], ln2_b[l])
        xp, st_p = _layer(xp, c_prompt, p, None)
        xs, st_s = _layer(xs, c_sample, p, (cache_fox_k[l], cache_fox_v[l], cache_fox_logf[l],
                                             state_mlstm_C[l], state_mlstm_n[l], state_mlstm_m[l]))
        new_p.append(st_p)
        new_s.append(st_s)

    def stk(lst, i):
        return jnp.stack([e[i] for e in lst], axis=0)

    return (xp, xs,
            stk(new_p, 0), stk(new_p, 1), stk(new_p, 2), stk(new_p, 3), stk(new_p, 4), stk(new_p, 5),
            stk(new_s, 0), stk(new_s, 1), stk(new_s, 2), stk(new_s, 3), stk(new_s, 4), stk(new_s, 5),
            stk(new_s, 6))
```

```python
import functools

import jax
import jax.numpy as jnp
from jax import lax
from jax.experimental import pallas as pl
from jax.experimental.pallas import tpu as pltpu

F32 = jnp.float32
BF16 = jnp.bfloat16

LN_EPS = 1e-5
HN_EPS = 1e-6
MLSTM_PROMPT_CHUNK = 128
ROW_TILE = 512
FOX_TILE = 512
LANE = 128
VMEM_LIMIT = 56 * 1024 * 1024


def _sigmoid(x):
    return 1.0 / (1.0 + jnp.exp(-x))


def _log_sigmoid(x):
    return jnp.minimum(x, 0.0) - jnp.log1p(jnp.exp(-jnp.abs(x)))


def _dot(a, b):
    return jnp.dot(a, b, preferred_element_type=F32)


def _dot_nt(a, b):
    return lax.dot_general(a, b, (((1,), (1,)), ((), ())), preferred_element_type=F32)


def _dot_tn(a, b):
    return lax.dot_general(a, b, (((0,), (0,)), ((), ())), preferred_element_type=F32)


def _dot_f32_by_01(x, u):
    hi = x.astype(BF16)
    r1 = x - hi.astype(F32)
    mid = r1.astype(BF16)
    lo = (r1 - mid.astype(F32)).astype(BF16)
    return _dot(hi, u) + _dot(mid, u) + _dot(lo, u)


def _layer_norm(x, g, b, eps):
    mu = jnp.mean(x, axis=-1, keepdims=True)
    xc = x - mu
    var = jnp.mean(xc * xc, axis=-1, keepdims=True)
    return xc * lax.rsqrt(var + eps) * g + b


def _params(*sem):
    return pltpu.CompilerParams(dimension_semantics=sem, vmem_limit_bytes=VMEM_LIMIT)


def _resident(shape):
    nd = len(shape)
    return pl.BlockSpec(shape, lambda *_: (0,) * nd, pipeline_mode=pl.Buffered(1))


def _mod_kernel(c_ref, w_ref, b_ref, o_ref):
    c = c_ref[...]
    s = (c * _sigmoid(c)).astype(BF16)
    o_ref[...] = _dot(s, w_ref[...].astype(BF16)) + b_ref[...]


def _modulation(c, w_ada, b_ada):
    nb, d = c.shape
    n = w_ada.shape[1]
    tn = n // 4 if (n // 4) % LANE == 0 else n
    return pl.pallas_call(
        _mod_kernel,
        out_shape=jax.ShapeDtypeStruct((nb, n), F32),
        grid=(n // tn,),
        in_specs=[pl.BlockSpec((nb, d), lambda j: (0, 0)),
                  pl.BlockSpec((d, tn), lambda j: (0, j)),
                  pl.BlockSpec((1, tn), lambda j: (0, j))],
        out_specs=pl.BlockSpec((nb, tn), lambda j: (0, j)),
        compiler_params=_params("parallel"),
        name="adaln_mod",
    )(c, w_ada, b_ada.reshape(1, n))


def _inproj_kernel(x_ref, mod_ref, w_ref, m_ref, fq_ref, fk_ref, fv_ref, fkb_ref, fvb_ref, g_ref, gate_ref,
                   *, dm4, df, dg2, fox_scale):
    bb, ts, d = x_ref.shape
    n = bb * ts
    mod = mod_ref[...]
    h = x_ref[...] * (1.0 + mod[:, 1:2, :]) + mod[:, 0:1, :]
    h = h.reshape(n, d).astype(BF16)

    def seg(c0, width):
        return _dot(h, w_ref[:, c0:c0 + width])

    m_ref[...] = seg(0, dm4).astype(BF16).reshape(bb, ts, dm4)
    c0 = dm4
    fq_ref[...] = (seg(c0, df) * fox_scale).astype(BF16).reshape(bb, ts, df)
    fk = seg(c0 + df, df)
    fk_ref[...] = fk.reshape(bb, ts, df)
    fkb_ref[...] = fk.astype(BF16).reshape(bb, ts, df)
    fv = seg(c0 + 2 * df, df)
    fv_ref[...] = fv.reshape(bb, ts, df)
    fvb_ref[...] = fv.astype(BF16).reshape(bb, ts, df)
    c0 += 3 * df
    g_ref[...] = seg(c0, dg2).reshape(bb, ts, dg2)
    gate_ref[...] = seg(c0 + dg2, LANE).reshape(bb, ts, LANE)


def _in_projection(x, mod, w_all, bb, ts, dm4, df, dg2, fox_scale):
    b, s, d = x.shape
    ncol = w_all.shape[1]
    row = lambda width: pl.BlockSpec((bb, ts, width), lambda i, j: (i, j, 0))
    shp = lambda width, dt: jax.ShapeDtypeStruct((b, s, width), dt)
    return pl.pallas_call(
        functools.partial(_inproj_kernel, dm4=dm4, df=df, dg2=dg2, fox_scale=fox_scale),
        out_shape=[shp(dm4, BF16), shp(df, BF16), shp(df, F32), shp(df, F32), shp(df, BF16), shp(df, BF16),
                   shp(dg2, F32), shp(LANE, F32)],
        grid=(b // bb, s // ts),
        in_specs=[row(d), pl.BlockSpec((bb, 6, d), lambda i, j: (i, 0, 0)), _resident((d, ncol))],
        out_specs=[row(dm4), row(df), row(df), row(df), row(df), row(df), row(dg2), row(LANE)],
        compiler_params=_params("parallel", "parallel"),
        name="in_proj",
    )(x, mod, w_all)


def _gates_kernel(g_ref, bias_ref, o_ref, *, hm, hf, lm):
    s = g_ref.shape[2]
    w = min(LANE, s)
    x = g_ref[0] + bias_ref[...]
    ls = _log_sigmoid(x)
    r = lax.broadcasted_iota(jnp.int32, (w, w), 0)
    c = lax.broadcasted_iota(jnp.int32, (w, w), 1)
    upper = r <= c
    u_all = upper.astype(BF16)
    u_chunk = jnp.logical_and(upper, (r // lm) == (c // lm)).astype(BF16)
    off = jnp.zeros((hf, 1), F32)
    for k in range(s // w):
        cols = slice(k * w, (k + 1) * w)
        li = x[0:hm, cols]
        b = _dot_f32_by_01(ls[hm:2 * hm, cols], u_chunk)
        logf = ls[2 * hm:2 * hm + hf, cols]
        fcum = _dot_f32_by_01(logf, u_all) + off
        off = fcum[:, w - 1:w]
        o_ref[0, 0:hm, cols] = li - b
        o_ref[0, hm:2 * hm, cols] = b
        o_ref[0, 2 * hm:2 * hm + hf, cols] = logf
        o_ref[0, 2 * hm + hf:2 * hm + 2 * hf, cols] = fcum


def _gate_prepass(gates_t, bias_col, hm, hf, lm):
    b, nr, s = gates_t.shape
    nout = 2 * hm + 2 * hf
    return pl.pallas_call(
        functools.partial(_gates_kernel, hm=hm, hf=hf, lm=lm),
        out_shape=jax.ShapeDtypeStruct((b, nout, s), F32),
        grid=(b,),
        in_specs=[pl.BlockSpec((1, nr, s), lambda i: (i, 0, 0)), pl.BlockSpec((nr, 1), lambda i: (0, 0))],
        out_specs=pl.BlockSpec((1, nout, s), lambda i: (i, 0, 0)),
        compiler_params=_params("parallel"),
        name="gate_prepass",
    )(gates_t, bias_col)


def _mlstm_kernel(qkvo_ref, gate_ref, c0_ref, m0_ref, ng_ref, hm_ref, c_ref, m_ref, *, nh, hd, lc):
    @pl.when(pl.program_id(1) == 0)
    def _():
        c_ref[...] = c0_ref[...]
        m_ref[...] = m0_ref[...]

    rows_per_step = qkvo_ref.shape[1]
    dm = nh * hd
    ksc = hd ** -0.5
    row = lax.broadcasted_iota(jnp.int32, (lc, lc), 0)
    col = lax.broadcasted_iota(jnp.int32, (lc, lc), 1)
    tri = col <= row
    eye = col == row
    ones_col = (lax.broadcasted_iota(jnp.int32, (lc, hd), 1) == 0).astype(BF16)
    neg_inf = jnp.float32(-jnp.inf)

    for ci in range(rows_per_step // lc):
        rows = slice(ci * lc, (ci + 1) * lc)
        m_old = m_ref[0]
        m_new = []
        for h in range(nh):
            lanes = slice(h * hd, (h + 1) * hd)
            q = qkvo_ref[0, rows, h * hd:(h + 1) * hd]
            k = qkvo_ref[0, rows, dm + h * hd:dm + (h + 1) * hd]
            v = qkvo_ref[0, rows, 2 * dm + h * hd:2 * dm + (h + 1) * hd]
            o = qkvo_ref[0, rows, 3 * dm + h * hd:3 * dm + (h + 1) * hd]
            a_row = gate_ref[0, h:h + 1, rows]
            b_row = gate_ref[0, nh + h:nh + h + 1, rows]
            m0 = m_old[:, h:h + 1]
            c_old = c_ref[0, h]

            a_full = jnp.broadcast_to(a_row, (lc, lc))
            g_col = jnp.maximum(jnp.max(jnp.where(tri, a_full, neg_inf), axis=-1, keepdims=True), m0)
            a_col = jnp.sum(jnp.where(eye, a_full, 0.0), axis=-1, keepdims=True)
            b_col = jnp.sum(jnp.where(eye, jnp.broadcast_to(b_row, (lc, lc)), 0.0), axis=-1, keepdims=True)
            w_intra = jnp.exp(jnp.where(tri, a_full - g_col, neg_inf))
            w_inter = jnp.exp(m0 - g_col)

            v_ext = jnp.concatenate([v, ones_col], axis=-1)
            sc = _dot_nt(q, k) * (w_intra * ksc)
            num = _dot(sc.astype(BF16), v_ext) + _dot(q, c_old.astype(BF16)) * w_inter
            den = num[:, hd:hd + 1]
            den = jnp.maximum(jnp.abs(den), jnp.exp(-(b_col + g_col)))
            hh = num[:, :hd] / den
            mu = jnp.mean(hh, axis=-1, keepdims=True)
            hc = hh - mu
            var = jnp.mean(hc * hc, axis=-1, keepdims=True)
            out = hc * lax.rsqrt(var + HN_EPS) * ng_ref[0:1, lanes] * _sigmoid(o.astype(F32))
            hm_ref[0, rows, lanes] = out.astype(BF16)

            g_end = g_col[lc - 1:lc, :]
            w_end = jnp.exp(a_col - g_end) * ksc
            vw = (v_ext.astype(F32) * w_end).astype(BF16)
            c_ref[0, h] = jnp.exp(m0 - g_end) * c_old + _dot_tn(k, vw)
            m_new.append(b_row[:, lc - 1:lc] + g_end)
        m_ref[0] = jnp.concatenate(m_new, axis=-1)


def _mlstm(qkvo, gate_rows, c0_ext, m0, norm_g, nh, hd, lc):
    b, s, d4 = qkvo.shape
    dm = nh * hd
    rows = max(lc, min(LANE, s))
    ngr = gate_rows.shape[1]
    return pl.pallas_call(
        functools.partial(_mlstm_kernel, nh=nh, hd=hd, lc=lc),
        out_shape=[jax.ShapeDtypeStruct((b, s, dm), BF16),
                   jax.ShapeDtypeStruct((b, nh, hd, 2 * hd), F32),
                   jax.ShapeDtypeStruct((b, 1, nh), F32)],
        grid=(b, s // rows),
        in_specs=[pl.BlockSpec((1, rows, d4), lambda i, j: (i, j, 0)),
                  pl.BlockSpec((1, ngr, rows), lambda i, j: (i, 0, j)),
                  pl.BlockSpec((1, nh, hd, 2 * hd), lambda i, j: (i, 0, 0, 0)),
                  pl.BlockSpec((1, 1, nh), lambda i, j: (i, 0, 0)),
                  pl.BlockSpec((1, dm), lambda i, j: (0, 0))],
        out_specs=[pl.BlockSpec((1, rows, dm), lambda i, j: (i, j, 0)),
                   pl.BlockSpec((1, nh, hd, 2 * hd), lambda i, j: (i, 0, 0, 0)),
                   pl.BlockSpec((1, 1, nh), lambda i, j: (i, 0, 0))],
        compiler_params=_params("parallel", "arbitrary"),
        name="mlstm_scan",
    )(qkvo, gate_rows, c0_ext, m0, norm_g.reshape(1, dm))


def _fox_prompt_kernel(q_ref, k_ref, v_ref, f_ref, o_ref, *, nh, hd, t, frow):
    i = pl.program_id(1)
    q0 = pl.multiple_of(i * t, t)
    row = lax.broadcasted_iota(jnp.int32, (t, t), 0)
    col = lax.broadcasted_iota(jnp.int32, (t, t), 1)
    causal = col <= row
    neg_inf = jnp.float32(-jnp.inf)

    for h in range(nh):
        lanes = slice(h * hd, (h + 1) * hd)
        fr = frow + h
        q = q_ref[0, :, lanes]
        f_q = f_ref[0, fr:fr + 1, pl.ds(q0, t)]
        f0 = f_q[:, 0:1]

        s = _dot_nt(q, k_ref[0, pl.ds(q0, t), lanes]) + (f0 - f_q)
        s = jnp.where(causal, s, neg_inf)
        m = jnp.max(s, axis=-1, keepdims=True)
        p = jnp.exp(s - m)
        l = jnp.sum(p, axis=-1, keepdims=True)
        acc = _dot(p.astype(BF16), v_ref[0, pl.ds(q0, t), lanes])

        def body(j, carry, lanes=lanes, fr=fr, q=q, f0=f0):
            m, l, acc = carry
            k0 = pl.multiple_of(j * t, t)
            s = _dot_nt(q, k_ref[0, pl.ds(k0, t), lanes]) + (f0 - f_ref[0, fr:fr + 1, pl.ds(k0, t)])
            m_new = jnp.maximum(m, jnp.max(s, axis=-1, keepdims=True))
            alpha = jnp.exp(m - m_new)
            p = jnp.exp(s - m_new)
            l = alpha * l + jnp.sum(p, axis=-1, keepdims=True)
            acc = alpha * acc + _dot(p.astype(BF16), v_ref[0, pl.ds(k0, t), lanes])
            return m_new, l, acc

        m, l, acc = lax.fori_loop(0, i, body, (m, l, acc))
        o_ref[0, :, lanes] = (acc / l).astype(BF16)


def _fox_prompt(q, k, v, gate_rows, nh, hd, frow):
    b, s, df = q.shape
    t = min(FOX_TILE, s)
    ngr = gate_rows.shape[1]
    return pl.pallas_call(
        functools.partial(_fox_prompt_kernel, nh=nh, hd=hd, t=t, frow=frow),
        out_shape=jax.ShapeDtypeStruct((b, s, df), BF16),
        grid=(b, s // t),
        in_specs=[pl.BlockSpec((1, t, df), lambda i, j: (i, j, 0)),
                  pl.BlockSpec((1, s, df), lambda i, j: (i, 0, 0)),
                  pl.BlockSpec((1, s, df), lambda i, j: (i, 0, 0)),
                  pl.BlockSpec((1, ngr, s), lambda i, j: (i, 0, 0))],
        out_specs=pl.BlockSpec((1, t, df), lambda i, j: (i, j, 0)),
        compiler_params=_params("parallel", "arbitrary"),
        name="fox_prompt",
    )(q, k, v, gate_rows)


def _fox_sample_kernel(q_ref, kn_ref, vn_ref, kc_ref, vc_ref, fc_ref, gate_ref, o_ref, *, nh, hd, frow):
    t = q_ref.shape[1]
    p_len = kc_ref.shape[1]
    w = min(LANE, p_len)
    r = lax.broadcasted_iota(jnp.int32, (w, w), 0)
    c = lax.broadcasted_iota(jnp.int32, (w, w), 1)
    u_all = (r <= c).astype(BF16)
    fc = fc_ref[0]
    parts = []
    off = jnp.zeros((nh, 1), F32)
    for kk in range(p_len // w):
        cs = _dot_f32_by_01(fc[:, kk * w:(kk + 1) * w], u_all) + off
        off = cs[:, w - 1:w]
        parts.append(cs)
    bias_c_all = off - jnp.concatenate(parts, axis=-1)
    bias_n_all = -gate_ref[0, frow:frow + nh, :]
    row = lax.broadcasted_iota(jnp.int32, (t, t), 0)
    col = lax.broadcasted_iota(jnp.int32, (t, t), 1)
    causal = col <= row
    neg_inf = jnp.float32(-jnp.inf)
    for h in range(nh):
        lanes = slice(h * hd, (h + 1) * hd)
        q = q_ref[0, :, lanes]
        s_c = _dot_nt(q, kc_ref[0, :, lanes].astype(BF16)) + bias_c_all[h:h + 1, :]
        s_n = _dot_nt(q, kn_ref[0, :, lanes]) + bias_n_all[h:h + 1, :]
        s_n = jnp.where(causal, s_n, neg_inf)
        m = jnp.maximum(jnp.max(s_c, axis=-1, keepdims=True), jnp.max(s_n, axis=-1, keepdims=True))
        p_c = jnp.exp(s_c - m)
        p_n = jnp.exp(s_n - m)
        l = jnp.sum(p_c, axis=-1, keepdims=True) + jnp.sum(p_n, axis=-1, keepdims=True)
        acc = _dot(p_c.astype(BF16), vc_ref[0, :, lanes].astype(BF16)) + _dot(p_n.astype(BF16), vn_ref[0, :, lanes])
        o_ref[0, :, lanes] = (acc / l).astype(BF16)


def _fox_sample(q, k_new, v_new, k_cache, v_cache, logf_cache_t, gate_rows, nh, hd, frow):
    b, t, df = q.shape
    p_len = k_cache.shape[1]
    ngr = gate_rows.shape[1]
    blk = lambda n, width: pl.BlockSpec((1, n, width), lambda i: (i, 0, 0))
    return pl.pallas_call(
        functools.partial(_fox_sample_kernel, nh=nh, hd=hd, frow=frow),
        out_shape=jax.ShapeDtypeStruct((b, t, df), BF16),
        grid=(b,),
        in_specs=[blk(t, df), blk(t, df), blk(t, df), blk(p_len, df), blk(p_len, df), blk(nh, p_len), blk(ngr, t)],
        out_specs=blk(t, df),
        compiler_params=_params("parallel"),
        name="fox_sample",
    )(q, k_new, v_new, k_cache, v_cache, logf_cache_t, gate_rows)


def _gmlp_kernel(g_ref, w_ref, bs_ref, lng_ref, lnb_ref, hg_ref, vn_ref, *, ng, hd, lc):
    bb, ts, d2 = g_ref.shape
    dg = d2 // 2
    n = bb * ts
    gall = g_ref[...].reshape(n, d2)
    gu = gall[:, :dg]
    vn = _layer_norm(gall[:, dg:], lng_ref[...], lnb_ref[...], LN_EPS)
    vn_ref[...] = vn.reshape(bb, ts, dg)
    row = lax.broadcasted_iota(jnp.int32, (lc, lc), 0)
    col = lax.broadcasted_iota(jnp.int32, (lc, lc), 1)
    tri = col <= row
    per_row = ts // lc
    for g in range(ng):
        lanes = slice(g * hd, (g + 1) * hd)
        wg = jnp.where(tri, w_ref[g, 0:lc, 0:lc], 0.0).astype(BF16)
        bcol = bs_ref[0:lc, g:g + 1]
        for ci in range(n // lc):
            rows = slice(ci * lc, (ci + 1) * lc)
            z = _dot(wg, vn[rows, lanes].astype(BF16)) + bcol
            bi, ri = ci // per_row, ci % per_row
            hg_ref[bi, ri * lc:(ri + 1) * lc, lanes] = (gu[rows, lanes] * z).astype(BF16)


def _gmlp(g_all, ws, bs_t, ln_g, ln_b, bb, ts, ng, hd, lc):
    b, s, d2 = g_all.shape
    dg = d2 // 2
    row = lambda width: pl.BlockSpec((bb, ts, width), lambda i, j: (i, j, 0))
    return pl.pallas_call(
        functools.partial(_gmlp_kernel, ng=ng, hd=hd, lc=lc),
        out_shape=[jax.ShapeDtypeStruct((b, s, dg), BF16), jax.ShapeDtypeStruct((b, s, dg), F32)],
        grid=(b // bb, s // ts),
        in_specs=[row(d2), _resident(ws.shape), _resident(bs_t.shape), _resident((1, dg)), _resident((1, dg))],
        out_specs=[row(dg), row(dg)],
        compiler_params=_params("parallel", "parallel"),
        name="gmlp_gate",
    )(g_all, ws, bs_t, ln_g.reshape(1, dg), ln_b.reshape(1, dg))


def _tail_kernel(x_ref, mod_ref, hm_ref, hf_ref, hg_ref, wo_ref, l1g_ref, l1b_ref, wg_ref, wu_ref, wd_ref,
                 l2g_ref, l2b_ref, o_ref, *, alpha, fchunk):
    bb, ts, d = x_ref.shape
    n = bb * ts
    dmm, dff, dgg = hm_ref.shape[2], hf_ref.shape[2], hg_ref.shape[2]
    mod = mod_ref[...]
    mix = (_dot(hm_ref[...].reshape(n, dmm), wo_ref[0:dmm, :])
           + _dot(hf_ref[...].reshape(n, dff), wo_ref[dmm:dmm + dff, :])
           + _dot(hg_ref[...].reshape(n, dgg), wo_ref[dmm + dff:dmm + dff + dgg, :]))
    x1 = _layer_norm(alpha * x_ref[...] + (1.0 + mod[:, 2:3, :]) * mix.reshape(bb, ts, d),
                     l1g_ref[...], l1b_ref[...], LN_EPS)
    h2 = (x1 * (1.0 + mod[:, 4:5, :]) + mod[:, 3:4, :]).reshape(n, d).astype(BF16)
    f = wg_ref.shape[1]
    y = jnp.zeros((n, d), F32)
    for f0 in range(0, f, fchunk):
        gt = _dot(h2, wg_ref[:, f0:f0 + fchunk])
        up = _dot(h2, wu_ref[:, f0:f0 + fchunk])
        act = (gt * _sigmoid(gt) * up).astype(BF16)
        y = y + _dot(act, wd_ref[f0:f0 + fchunk, :])
    o_ref[...] = _layer_norm(alpha * x1 + (1.0 + mod[:, 5:6, :]) * y.reshape(bb, ts, d),
                             l2g_ref[...], l2b_ref[...], LN_EPS)


def _ffn_chunk(f):
    for c in (704, 512, 384, 256, 128):
        if f % c == 0:
            return c
    return f


def _layer_tail(x, mod, hm, hf, hg, wo, l1g, l1b, wg, wu, wd, l2g, l2b, bb, ts, alpha):
    b, s, d = x.shape
    f = wg.shape[1]
    row = lambda width: pl.BlockSpec((bb, ts, width), lambda i, j: (i, j, 0))
    vec = lambda a: a.reshape(1, d)
    return pl.pallas_call(
        functools.partial(_tail_kernel, alpha=alpha, fchunk=_ffn_chunk(f)),
        out_shape=jax.ShapeDtypeStruct((b, s, d), F32),
        grid=(b // bb, s // ts),
        in_specs=[row(d), pl.BlockSpec((bb, 6, d), lambda i, j: (i, 0, 0)),
                  row(hm.shape[2]), row(hf.shape[2]), row(hg.shape[2]),
                  _resident(wo.shape), _resident((1, d)), _resident((1, d)),
                  _resident(wg.shape), _resident(wu.shape), _resident(wd.shape),
                  _resident((1, d)), _resident((1, d))],
        out_specs=row(d),
        compiler_params=_params("parallel", "parallel"),
        name="layer_tail",
    )(x, mod, hm, hf, hg, wo, vec(l1g), vec(l1b), wg, wu, wd, vec(l2g), vec(l2b))


def _tile(b, s):
    if s >= ROW_TILE:
        return 1, ROW_TILE
    bb = max(1, min(b, ROW_TILE // s))
    while b % bb:
        bb -= 1
    return bb, s


def _layer(x, mod, p, cache, dims):
    nh_m, nh_f, ng, hd = dims
    b, s, d = x.shape
    dm, df, dg = nh_m * hd, nh_f * hd, ng * hd
    bb, ts = _tile(b, s)
    alpha = p["alpha"]

    m_all, fq, fk, fv, fkb, fvb, g_all, gates = _in_projection(
        x, mod, p["w_in"], bb, ts, 4 * dm, df, 2 * dg, hd ** -0.5)

    nrow = 2 * nh_m + nh_f
    gates_t = jnp.transpose(gates[:, :, :nrow], (0, 2, 1))
    lm = s if cache is not None else min(MLSTM_PROMPT_CHUNK, s)
    gate_rows = _gate_prepass(gates_t, p["gate_bias"], nh_m, nh_f, lm)
    logf = jnp.transpose(gate_rows[:, 2 * nh_m:2 * nh_m + nh_f, :], (0, 2, 1))
    frow = 2 * nh_m + nh_f

    if cache is None:
        c0 = jnp.zeros((b, nh_m, hd, 2 * hd), F32)
        m0 = jnp.full((b, 1, nh_m), -jnp.inf, F32)
    else:
        k_c, v_c, logf_c, c_st, n_st, m_st = cache
        c0 = jnp.concatenate([c_st, n_st[..., None], jnp.zeros((b, nh_m, hd, hd - 1), F32)], axis=-1)
        m0 = m_st.reshape(b, 1, nh_m)
    hm, c_ext, m1 = _mlstm(m_all, gate_rows, c0, m0, p["mlstm_norm_g"], nh_m, hd, lm)

    if cache is None:
        hf = _fox_prompt(fq, fkb, fvb, gate_rows, nh_f, hd, frow)
    else:
        p_len = k_c.shape[1]
        hf = _fox_sample(fq, fkb, fvb, k_c.reshape(b, p_len, df), v_c.reshape(b, p_len, df),
                         jnp.transpose(logf_c, (0, 2, 1)), gate_rows, nh_f, hd, frow)

    lc = min(s, p["gmlp_ws"].shape[-1])
    hg, vn = _gmlp(g_all, p["gmlp_ws"], p["gmlp_bs_t"], p["gmlp_ln_g"], p["gmlp_ln_b"], bb, ts, ng, hd, lc)

    x_out = _layer_tail(x, mod, hm, hf, hg, p["w_o"], p["ln1_g"], p["ln1_b"], p["w_gate"], p["w_up"],
                        p["w_down"], p["ln2_g"], p["ln2_b"], bb, ts, alpha)
    new = (fk.reshape(b, s, nh_f, hd), fv.reshape(b, s, nh_f, hd), logf,
           c_ext[..., :hd], c_ext[..., hd], m1.reshape(b, nh_m), vn)
    return x_out, new


def kernel(x_prompt, x_sample, cache_fox_k, cache_fox_v, cache_fox_logf, state_mlstm_C, state_mlstm_n, state_mlstm_m, c_prompt, c_sample, w_ada, b_ada, w_in, b_mlstm_i, b_mlstm_f, mlstm_norm_g, b_fox_f, gmlp_ln_g, gmlp_ln_b, gmlp_ws, gmlp_bs, w_o, ln1_g, ln1_b, w_gate, w_up, w_down, ln2_g, ln2_b):
    depth = w_in.shape[0]
    nh_m = b_mlstm_i.shape[1]
    nh_f = b_fox_f.shape[1]
    ng = gmlp_ws.shape[1]
    hd = cache_fox_k.shape[-1]
    dm, df, dg = nh_m * hd, nh_f * hd, ng * hd
    d = x_prompt.shape[-1]
    bp = x_prompt.shape[0]
    alpha = (2 * depth) ** 0.25

    o_mi = 4 * dm
    o_fq = o_mi + 2 * nh_m
    o_ff = o_fq + 3 * df
    o_gu = o_ff + nh_f
    npad = LANE - (2 * nh_m + nh_f)

    xp, xs = x_prompt, x_sample
    c_all = jnp.concatenate([c_prompt, c_sample], axis=0)
    new_p, new_s = [], []
    for l in range(depth):
        wl = w_in[l]
        w_all = jnp.concatenate(
            [wl[:, :o_mi], wl[:, o_fq:o_ff], wl[:, o_gu:o_gu + 2 * dg], wl[:, o_mi:o_fq], wl[:, o_ff:o_gu],
             jnp.zeros((d, npad), wl.dtype)], axis=1).astype(BF16)
        p = dict(
            alpha=alpha, w_in=w_all,
            gate_bias=jnp.concatenate([b_mlstm_i[l], b_mlstm_f[l], b_fox_f[l]]).reshape(-1, 1),
            mlstm_norm_g=mlstm_norm_g[l], gmlp_ln_g=gmlp_ln_g[l], gmlp_ln_b=gmlp_ln_b[l],
            gmlp_ws=gmlp_ws[l], gmlp_bs_t=jnp.transpose(gmlp_bs[l]),
            w_o=w_o[l].astype(BF16), ln1_g=ln1_g[l], ln1_b=ln1_b[l],
            w_gate=w_gate[l].astype(BF16), w_up=w_up[l].astype(BF16), w_down=w_down[l].astype(BF16),
            ln2_g=ln2_g[l], ln2_b=ln2_b[l])
        mod = _modulation(c_all, w_ada[l], b_ada[l]).reshape(c_all.shape[0], 6, d)
        xp, st_p = _layer(xp, mod[:bp], p, None, (nh_m, nh_f, ng, hd))
        xs, st_s = _layer(xs, mod[bp:], p, (cache_fox_k[l], cache_fox_v[l], cache_fox_logf[l],
                                            state_mlstm_C[l], state_mlstm_n[l], state_mlstm_m[l]),
                          (nh_m, nh_f, ng, hd))
        new_p.append(st_p)
        new_s.append(st_s)

    def stk(lst, i):
        return jnp.stack([e[i] for e in lst], axis=0)

    return (xp, xs,
            stk(new_p, 0), stk(new_p, 1), stk(new_p, 2), stk(new_p, 3), stk(new_p, 4), stk(new_p, 5),
            stk(new_s, 0), stk(new_s, 1), stk(new_s, 2), stk(new_s, 3), stk(new_s, 4), stk(new_s, 5),
            stk(new_s, 6))
```

```python
import functools

import jax
import jax.numpy as jnp
from jax import lax
from jax.experimental import pallas as pl
from jax.experimental.pallas import tpu as pltpu

F32 = jnp.float32
BF16 = jnp.bfloat16

LOG2E = 1.4426950408889634
LN_EPS = 1e-5
HN_EPS = 1e-6
MLSTM_PROMPT_CHUNK = 128
MLSTM_BATCH_ROWS = 4
ROW_TILE = 512
FOX_TILE = 512
FOX_ONES_ROWS = 16
LANE = 128
VMEM_LIMIT = 56 * 1024 * 1024


def _sigmoid(x):
    return 1.0 / (1.0 + jnp.exp(-x))


def _log_sigmoid(x):
    return jnp.minimum(x, 0.0) - jnp.log1p(jnp.exp(-jnp.abs(x)))


def _dot(a, b):
    return jnp.dot(a, b, preferred_element_type=F32)


def _dot_nt(a, b):
    return lax.dot_general(a, b, (((1,), (1,)), ((), ())), preferred_element_type=F32)


def _dot_tn(a, b):
    return lax.dot_general(a, b, (((0,), (0,)), ((), ())), preferred_element_type=F32)


def _dot_f32_by_01(x, u):
    hi = x.astype(BF16)
    r1 = x - hi.astype(F32)
    mid = r1.astype(BF16)
    lo = (r1 - mid.astype(F32)).astype(BF16)
    return _dot(hi, u) + _dot(mid, u) + _dot(lo, u)


def _layer_norm(x, g, b, eps):
    mu = jnp.mean(x, axis=-1, keepdims=True)
    xc = x - mu
    var = jnp.mean(xc * xc, axis=-1, keepdims=True)
    return xc * lax.rsqrt(var + eps) * g + b


def _params(*sem):
    return pltpu.CompilerParams(dimension_semantics=sem, vmem_limit_bytes=VMEM_LIMIT)


def _resident(shape):
    nd = len(shape)
    return pl.BlockSpec(shape, lambda *_: (0,) * nd, pipeline_mode=pl.Buffered(1))


def _mod_kernel(c_ref, w_ref, b_ref, o_ref):
    c = c_ref[...]
    s = (c * _sigmoid(c)).astype(BF16)
    o_ref[...] = _dot(s, w_ref[...].astype(BF16)) + b_ref[...]


def _modulation(c, w_ada, b_ada):
    nb, d = c.shape
    n = w_ada.shape[1]
    tn = n // 4 if (n // 4) % LANE == 0 else n
    return pl.pallas_call(
        _mod_kernel,
        out_shape=jax.ShapeDtypeStruct((nb, n), F32),
        grid=(n // tn,),
        in_specs=[pl.BlockSpec((nb, d), lambda j: (0, 0)),
                  pl.BlockSpec((d, tn), lambda j: (0, j)),
                  pl.BlockSpec((1, tn), lambda j: (0, j))],
        out_specs=pl.BlockSpec((nb, tn), lambda j: (0, j)),
        compiler_params=_params("parallel"),
        name="adaln_mod",
    )(c, w_ada, b_ada.reshape(1, n))


def _inproj_kernel(x_ref, mod_ref, w_ref, m_ref, fq_ref, fk_ref, fv_ref, fkb_ref, fvb_ref, g_ref, gate_ref,
                   *, dm4, df, dg2, fox_scale):
    bb, ts, d = x_ref.shape
    n = bb * ts
    mod = mod_ref[...]
    h = x_ref[...] * (1.0 + mod[:, 1:2, :]) + mod[:, 0:1, :]
    h = h.reshape(n, d).astype(BF16)

    def seg(c0, width):
        return _dot(h, w_ref[:, c0:c0 + width])

    m_ref[...] = seg(0, dm4).astype(BF16).reshape(bb, ts, dm4)
    c0 = dm4
    fq_ref[...] = (seg(c0, df) * fox_scale).astype(BF16).reshape(bb, ts, df)
    fk = seg(c0 + df, df)
    fk_ref[...] = fk.reshape(bb, ts, df)
    fkb_ref[...] = fk.astype(BF16).reshape(bb, ts, df)
    fv = seg(c0 + 2 * df, df)
    fv_ref[...] = fv.reshape(bb, ts, df)
    fvb_ref[...] = fv.astype(BF16).reshape(bb, ts, df)
    c0 += 3 * df
    g_ref[...] = seg(c0, dg2).reshape(bb, ts, dg2)
    gate_ref[...] = seg(c0 + dg2, LANE).reshape(bb, ts, LANE)


def _in_projection(x, mod, w_all, bb, ts, dm4, df, dg2, fox_scale):
    b, s, d = x.shape
    ncol = w_all.shape[1]
    row = lambda width: pl.BlockSpec((bb, ts, width), lambda i, j: (i, j, 0))
    shp = lambda width, dt: jax.ShapeDtypeStruct((b, s, width), dt)
    return pl.pallas_call(
        functools.partial(_inproj_kernel, dm4=dm4, df=df, dg2=dg2, fox_scale=fox_scale),
        out_shape=[shp(dm4, BF16), shp(df, BF16), shp(df, F32), shp(df, F32), shp(df, BF16), shp(df, BF16),
                   shp(dg2, F32), shp(LANE, F32)],
        grid=(b // bb, s // ts),
        in_specs=[row(d), pl.BlockSpec((bb, 6, d), lambda i, j: (i, 0, 0)), _resident((d, ncol))],
        out_specs=[row(dm4), row(df), row(df), row(df), row(df), row(df), row(dg2), row(LANE)],
        compiler_params=_params("parallel", "parallel"),
        name="in_proj",
    )(x, mod, w_all)


def _gates_kernel(g_ref, bias_ref, o_ref, *, hm, hf, lm):
    s = g_ref.shape[2]
    w = min(LANE, s)
    x = g_ref[0] + bias_ref[...]
    ls = _log_sigmoid(x)
    r = lax.broadcasted_iota(jnp.int32, (w, w), 0)
    c = lax.broadcasted_iota(jnp.int32, (w, w), 1)
    upper = r <= c
    u_all = upper.astype(BF16)
    u_chunk = jnp.logical_and(upper, (r // lm) == (c // lm)).astype(BF16)
    off = jnp.zeros((hf, 1), F32)
    for k in range(s // w):
        cols = slice(k * w, (k + 1) * w)
        li = x[0:hm, cols]
        b = _dot_f32_by_01(ls[hm:2 * hm, cols], u_chunk)
        logf = ls[2 * hm:2 * hm + hf, cols]
        fcum = _dot_f32_by_01(logf, u_all) + off
        off = fcum[:, w - 1:w]
        o_ref[0, 0:hm, cols] = li - b
        o_ref[0, hm:2 * hm, cols] = b
        o_ref[0, 2 * hm:2 * hm + hf, cols] = logf
        o_ref[0, 2 * hm + hf:2 * hm + 2 * hf, cols] = fcum


def _gate_prepass(gates_t, bias_col, hm, hf, lm):
    b, nr, s = gates_t.shape
    nout = 2 * hm + 2 * hf
    return pl.pallas_call(
        functools.partial(_gates_kernel, hm=hm, hf=hf, lm=lm),
        out_shape=jax.ShapeDtypeStruct((b, nout, s), F32),
        grid=(b,),
        in_specs=[pl.BlockSpec((1, nr, s), lambda i: (i, 0, 0)), pl.BlockSpec((nr, 1), lambda i: (0, 0))],
        out_specs=pl.BlockSpec((1, nout, s), lambda i: (i, 0, 0)),
        compiler_params=_params("parallel"),
        name="gate_prepass",
    )(gates_t, bias_col)


def _dot_f32_2(x, w):
    hi = x.astype(BF16)
    lo = (x - hi.astype(F32)).astype(BF16)
    return _dot(hi, w) + _dot(lo, w)


def _mlstm_kernel(qkvo_ref, gate_ref, gcol_ref, c0_ref, n0_ref, m0_ref, ng_ref, hm_ref, c_ref, n_ref, m_ref,
                  *, nh, hd, lc):
    @pl.when(pl.program_id(1) == 0)
    def _():
        c_ref[...] = c0_ref[...]
        n_ref[...] = n0_ref[...]
        m_ref[...] = m0_ref[...]

    nb = qkvo_ref.shape[0]
    rows_per_step = qkvo_ref.shape[1]
    dm = nh * hd
    ksc = hd ** -0.5
    row = lax.broadcasted_iota(jnp.int32, (lc, lc), 0)
    col = lax.broadcasted_iota(jnp.int32, (lc, lc), 1)
    tri = col <= row
    neg_inf = jnp.float32(-jnp.inf)
    ones_v = jnp.ones((lc, hd), BF16)
    avg = jnp.full((hd, hd), 1.0 / hd, BF16)

    chains = [(bi, h) for bi in range(nb) for h in range(nh)]
    for ci in range(rows_per_step // lc):
        rows = slice(ci * lc, (ci + 1) * lc)
        m_olds = [m_ref[bi] for bi in range(nb)]
        st = {}
        for bi, h in chains:
            q = qkvo_ref[bi, rows, h * hd:(h + 1) * hd]
            k = qkvo_ref[bi, rows, dm + h * hd:dm + (h + 1) * hd]
            a_row = gate_ref[bi, h:h + 1, rows]
            m0 = m_olds[bi][:, h:h + 1]
            c_old = c_ref[bi, h]
            n_old = n_ref[bi, h]
            a_full = jnp.broadcast_to(a_row, (lc, lc))
            g_col = jnp.maximum(jnp.max(jnp.where(tri, a_full, neg_inf), axis=-1, keepdims=True), m0)
            st[bi, h] = dict(k=k, m0=m0, c_old=c_old, n_old=n_old, a_full=a_full, g_col=g_col,
                             qk=_dot_nt(q, k), qc=_dot(q, c_old.astype(BF16)), qn=_dot(q, n_old.astype(BF16)))
        for bi, h in chains:
            c = st[bi, h]
            v = qkvo_ref[bi, rows, 2 * dm + h * hd:2 * dm + (h + 1) * hd]
            g_b = jnp.broadcast_to(c["g_col"], (lc, max(lc, hd)))
            a_b = jnp.broadcast_to(gcol_ref[bi, rows, h:h + 1], (lc, hd))
            b_b = jnp.broadcast_to(gcol_ref[bi, rows, nh + h:nh + h + 1], (lc, hd))
            g_h = g_b[:, :hd]
            w_intra = jnp.exp(jnp.where(tri, c["a_full"] - g_b[:, :lc], neg_inf))
            c["sc"] = (c["qk"] * (w_intra * ksc)).astype(BF16)
            c["w_inter"] = jnp.exp(c["m0"] - g_h)
            c["floor"] = jnp.exp(-(b_b + g_h))
            c["g_end"] = c["g_col"][lc - 1:lc, :]
            w_end = jnp.exp(a_b - c["g_end"]) * ksc
            c["w_end"] = w_end.astype(BF16)
            c["vw"] = (v.astype(F32) * w_end).astype(BF16)
            c["v"] = v
        m_new = [[] for _ in range(nb)]
        for bi, h in chains:
            c = st[bi, h]
            decay = jnp.exp(c["m0"] - c["g_end"])
            c_ref[bi, h] = decay * c["c_old"] + _dot_tn(c["k"], c["vw"])
            n_ref[bi, h] = decay * c["n_old"] + _dot_tn(c["k"], c["w_end"])
            b_row = gate_ref[bi, nh + h:nh + h + 1, rows]
            m_new[bi].append(b_row[:, lc - 1:lc] + c["g_end"])
        for bi in range(nb):
            m_ref[bi] = jnp.concatenate(m_new[bi], axis=-1)
        for bi, h in chains:
            c = st[bi, h]
            num = _dot(c["sc"], c["v"]) + c["qc"] * c["w_inter"]
            den = _dot(c["sc"], ones_v) + c["qn"] * c["w_inter"]
            c["hh"] = num / jnp.maximum(jnp.abs(den), c["floor"])
        for bi, h in chains:
            c = st[bi, h]
            c["hc"] = c["hh"] - _dot_f32_2(c["hh"], avg)
        for bi, h in chains:
            c = st[bi, h]
            c["var"] = _dot_f32_2(c["hc"] * c["hc"], avg)
        for bi, h in chains:
            c = st[bi, h]
            lanes = slice(h * hd, (h + 1) * hd)
            o = qkvo_ref[bi, rows, 3 * dm + h * hd:3 * dm + (h + 1) * hd]
            out = c["hc"] * lax.rsqrt(c["var"] + HN_EPS) * ng_ref[0:1, lanes] * _sigmoid(o.astype(F32))
            hm_ref[bi, rows, lanes] = out.astype(BF16)


def _mlstm(qkvo, gate_rows, gate_cols, c0, n0_rep, m0, norm_g, nh, hd, lc):
    b, s, d4 = qkvo.shape
    dm = nh * hd
    rows = max(lc, min(LANE, s))
    ngr = gate_rows.shape[1]
    ngc = gate_cols.shape[2]
    nb = MLSTM_BATCH_ROWS if b % MLSTM_BATCH_ROWS == 0 else 1
    state = pl.BlockSpec((nb, nh, hd, hd), lambda i, j: (i, 0, 0, 0))
    mspec = pl.BlockSpec((nb, 1, nh), lambda i, j: (i, 0, 0))
    return pl.pallas_call(
        functools.partial(_mlstm_kernel, nh=nh, hd=hd, lc=lc),
        out_shape=[jax.ShapeDtypeStruct((b, s, dm), BF16),
                   jax.ShapeDtypeStruct((b, nh, hd, hd), F32),
                   jax.ShapeDtypeStruct((b, nh, hd, hd), F32),
                   jax.ShapeDtypeStruct((b, 1, nh), F32)],
        grid=(b // nb, s // rows),
        in_specs=[pl.BlockSpec((nb, rows, d4), lambda i, j: (i, j, 0)),
                  pl.BlockSpec((nb, ngr, rows), lambda i, j: (i, 0, j)),
                  pl.BlockSpec((nb, rows, ngc), lambda i, j: (i, j, 0)),
                  state, state, mspec,
                  pl.BlockSpec((1, dm), lambda i, j: (0, 0))],
        out_specs=[pl.BlockSpec((nb, rows, dm), lambda i, j: (i, j, 0)), state, state, mspec],
        compiler_params=_params("parallel", "arbitrary"),
        name="mlstm_scan",
    )(qkvo, gate_rows, gate_cols, c0, n0_rep, m0, norm_g.reshape(1, dm))


def _fox_prep_kernel(q_ref, k_ref, v_ref, f_ref, qa_ref, ka_ref, vt_ref, *, nh, hd):
    ts = q_ref.shape[1]
    lane = lax.broadcasted_iota(jnp.int32, (ts, hd), 1)
    ones3 = (lane < 3).astype(BF16)
    nf = f_ref[0] * (-LOG2E)
    q_parts, k_parts = [], []
    for h in range(nh):
        lanes = slice(h * hd, (h + 1) * hd)
        x = nf[:, h:h + 1]
        hi = x.astype(BF16).astype(F32)
        r1 = x - hi
        mid = r1.astype(BF16).astype(F32)
        lo = r1 - mid
        extra = jnp.where(lane == 0, hi, jnp.where(lane == 1, mid, jnp.where(lane == 2, lo, 0.0)))
        q_parts += [q_ref[0, :, lanes], ones3]
        k_parts += [k_ref[0, :, lanes], extra.astype(BF16)]
    qa_ref[0] = jnp.concatenate(q_parts, axis=-1)
    ka_ref[0] = jnp.concatenate(k_parts, axis=-1)
    vt = jnp.transpose(v_ref[0].astype(F32))
    ones = jnp.ones((FOX_ONES_ROWS, ts), F32)
    v_parts = []
    for h in range(nh):
        v_parts += [vt[h * hd:(h + 1) * hd, :], ones]
    vt_ref[0] = jnp.concatenate(v_parts, axis=0).astype(BF16)


def _fox_prep(q, k, v, f_col, nh, hd):
    b, s, df = q.shape
    ts = min(ROW_TILE, s)
    dv = nh * (hd + FOX_ONES_ROWS)
    row = lambda width: pl.BlockSpec((1, ts, width), lambda i, j: (i, j, 0))
    return pl.pallas_call(
        functools.partial(_fox_prep_kernel, nh=nh, hd=hd),
        out_shape=[jax.ShapeDtypeStruct((b, s, 2 * df), BF16), jax.ShapeDtypeStruct((b, s, 2 * df), BF16),
                   jax.ShapeDtypeStruct((b, dv, s), BF16)],
        grid=(b, s // ts),
        in_specs=[row(df), row(df), row(df), row(nh)],
        out_specs=[row(2 * df), row(2 * df), pl.BlockSpec((1, dv, ts), lambda i, j: (i, 0, j))],
        compiler_params=_params("parallel", "parallel"),
        name="fox_prep",
    )(q, k, v, f_col)


def _fox_prompt_kernel(qa_ref, ka_ref, vt_ref, o_ref, m_sc, acc_sc, *, nh, hd, t):
    i = pl.program_id(1)
    q0 = pl.multiple_of(i * t, t)
    w = 2 * hd
    hv = hd + FOX_ONES_ROWS
    key = lax.broadcasted_iota(jnp.int32, (t, t), 0)
    qry = lax.broadcasted_iota(jnp.int32, (t, t), 1)
    causal = key <= qry
    neg_inf = jnp.float32(-jnp.inf)

    def scores(k0):
        return [_dot_nt(ka_ref[0, pl.ds(k0, t), h * w:(h + 1) * w], qa_ref[0, :, h * w:(h + 1) * w])
                for h in range(nh)]

    def weighted_values(k0, ps):
        return [_dot(vt_ref[0, h * hv:(h + 1) * hv, pl.ds(k0, t)], ps[h]) for h in range(nh)]

    ss = scores(q0)
    ms, ps = [], []
    for h in range(nh):
        s = jnp.where(causal, ss[h], neg_inf)
        m = jnp.max(s, axis=0, keepdims=True)
        ms.append(m)
        ps.append(jnp.exp2(s - m).astype(BF16))
    m_sc[...] = jnp.concatenate(ms, axis=0)
    pvs = weighted_values(q0, ps)
    for h in range(nh):
        acc_sc[h * hv:(h + 1) * hv, :] = pvs[h]

    def body(j, carry):
        k0 = pl.multiple_of(j * t, t)
        ss = scores(k0)
        m_all = m_sc[...]
        ms, ps, alphas = [], [], []
        for h in range(nh):
            m_old = m_all[h:h + 1, :]
            m_new = jnp.maximum(m_old, jnp.max(ss[h], axis=0, keepdims=True))
            alphas.append(jnp.exp2(m_old - m_new))
            ps.append(jnp.exp2(ss[h] - m_new).astype(BF16))
            ms.append(m_new)
        m_sc[...] = jnp.concatenate(ms, axis=0)
        pvs = weighted_values(k0, ps)
        for h in range(nh):
            rows = slice(h * hv, (h + 1) * hv)
            acc_sc[rows, :] = alphas[h] * acc_sc[rows, :] + pvs[h]
        return carry

    lax.fori_loop(0, i, body, 0)
    for h in range(nh):
        o_ref[0, h * hd:(h + 1) * hd, :] = (acc_sc[h * hv:h * hv + hd, :]
                                            / acc_sc[h * hv + hd:h * hv + hd + 1, :]).astype(BF16)


def _fox_prompt(qa, ka, vt, nh, hd):
    b, s, w2 = qa.shape
    df = nh * hd
    dv = vt.shape[1]
    t = min(FOX_TILE, s)
    return pl.pallas_call(
        functools.partial(_fox_prompt_kernel, nh=nh, hd=hd, t=t),
        out_shape=jax.ShapeDtypeStruct((b, df, s), BF16),
        grid=(b, s // t),
        in_specs=[pl.BlockSpec((1, t, w2), lambda i, j: (i, j, 0)),
                  pl.BlockSpec((1, s, w2), lambda i, j: (i, 0, 0)),
                  pl.BlockSpec((1, dv, s), lambda i, j: (i, 0, 0))],
        out_specs=pl.BlockSpec((1, df, t), lambda i, j: (i, 0, j)),
        scratch_shapes=[pltpu.VMEM((nh, t), F32), pltpu.VMEM((dv, t), F32)],
        compiler_params=_params("parallel", "arbitrary"),
        name="fox_prompt",
    )(qa, ka, vt)


def _fox_sample_kernel(q_ref, kn_ref, vn_ref, kc_ref, vc_ref, fc_ref, gate_ref, o_ref, *, nh, hd, frow):
    t = q_ref.shape[1]
    p_len = kc_ref.shape[1]
    w = min(LANE, p_len)
    r = lax.broadcasted_iota(jnp.int32, (w, w), 0)
    c = lax.broadcasted_iota(jnp.int32, (w, w), 1)
    u_all = (r <= c).astype(BF16)
    fc = fc_ref[0]
    parts = []
    off = jnp.zeros((nh, 1), F32)
    for kk in range(p_len // w):
        cs = _dot_f32_by_01(fc[:, kk * w:(kk + 1) * w], u_all) + off
        off = cs[:, w - 1:w]
        parts.append(cs)
    bias_c_all = off - jnp.concatenate(parts, axis=-1)
    bias_n_all = -gate_ref[0, frow:frow + nh, :]
    row = lax.broadcasted_iota(jnp.int32, (t, t), 0)
    col = lax.broadcasted_iota(jnp.int32, (t, t), 1)
    causal = col <= row
    neg_inf = jnp.float32(-jnp.inf)
    for h in range(nh):
        lanes = slice(h * hd, (h + 1) * hd)
        q = q_ref[0, :, lanes]
        s_c = _dot_nt(q, kc_ref[0, :, lanes].astype(BF16)) + bias_c_all[h:h + 1, :]
        s_n = _dot_nt(q, kn_ref[0, :, lanes]) + bias_n_all[h:h + 1, :]
        s_n = jnp.where(causal, s_n, neg_inf)
        m = jnp.maximum(jnp.max(s_c, axis=-1, keepdims=True), jnp.max(s_n, axis=-1, keepdims=True))
        p_c = jnp.exp(s_c - m)
        p_n = jnp.exp(s_n - m)
        l = jnp.sum(p_c, axis=-1, keepdims=True) + jnp.sum(p_n, axis=-1, keepdims=True)
        acc = _dot(p_c.astype(BF16), vc_ref[0, :, lanes].astype(BF16)) + _dot(p_n.astype(BF16), vn_ref[0, :, lanes])
        o_ref[0, :, lanes] = (acc / l).astype(BF16)


def _fox_sample(q, k_new, v_new, k_cache, v_cache, logf_cache_t, gate_rows, nh, hd, frow):
    b, t, df = q.shape
    p_len = k_cache.shape[1]
    ngr = gate_rows.shape[1]
    blk = lambda n, width: pl.BlockSpec((1, n, width), lambda i: (i, 0, 0))
    return pl.pallas_call(
        functools.partial(_fox_sample_kernel, nh=nh, hd=hd, frow=frow),
        out_shape=jax.ShapeDtypeStruct((b, t, df), BF16),
        grid=(b,),
        in_specs=[blk(t, df), blk(t, df), blk(t, df), blk(p_len, df), blk(p_len, df), blk(nh, p_len), blk(ngr, t)],
        out_specs=blk(t, df),
        compiler_params=_params("parallel"),
        name="fox_sample",
    )(q, k_new, v_new, k_cache, v_cache, logf_cache_t, gate_rows)


def _gmlp_kernel(g_ref, w_ref, bs_ref, lng_ref, lnb_ref, hg_ref, vn_ref, *, ng, hd, lc):
    bb, ts, d2 = g_ref.shape
    dg = d2 // 2
    n = bb * ts
    gall = g_ref[...].reshape(n, d2)
    gu = gall[:, :dg]
    vn = _layer_norm(gall[:, dg:], lng_ref[...], lnb_ref[...], LN_EPS)
    vn_ref[...] = vn.reshape(bb, ts, dg)
    row = lax.broadcasted_iota(jnp.int32, (lc, lc), 0)
    col = lax.broadcasted_iota(jnp.int32, (lc, lc), 1)
    tri = col <= row
    per_row = ts // lc
    for g in range(ng):
        lanes = slice(g * hd, (g + 1) * hd)
        wg = jnp.where(tri, w_ref[g, 0:lc, 0:lc], 0.0).astype(BF16)
        bcol = bs_ref[0:lc, g:g + 1]
        for ci in range(n // lc):
            rows = slice(ci * lc, (ci + 1) * lc)
            z = _dot(wg, vn[rows, lanes].astype(BF16)) + bcol
            bi, ri = ci // per_row, ci % per_row
            hg_ref[bi, ri * lc:(ri + 1) * lc, lanes] = (gu[rows, lanes] * z).astype(BF16)


def _gmlp(g_all, ws, bs_t, ln_g, ln_b, bb, ts, ng, hd, lc):
    b, s, d2 = g_all.shape
    dg = d2 // 2
    row = lambda width: pl.BlockSpec((bb, ts, width), lambda i, j: (i, j, 0))
    return pl.pallas_call(
        functools.partial(_gmlp_kernel, ng=ng, hd=hd, lc=lc),
        out_shape=[jax.ShapeDtypeStruct((b, s, dg), BF16), jax.ShapeDtypeStruct((b, s, dg), F32)],
        grid=(b // bb, s // ts),
        in_specs=[row(d2), _resident(ws.shape), _resident(bs_t.shape), _resident((1, dg)), _resident((1, dg))],
        out_specs=[row(dg), row(dg)],
        compiler_params=_params("parallel", "parallel"),
        name="gmlp_gate",
    )(g_all, ws, bs_t, ln_g.reshape(1, dg), ln_b.reshape(1, dg))


def _tail_kernel(x_ref, mod_ref, hm_ref, hf_ref, hg_ref, wo_ref, l1g_ref, l1b_ref, wg_ref, wu_ref, wd_ref,
                 l2g_ref, l2b_ref, o_ref, *, alpha, fchunk, hf_transposed):
    bb, ts, d = x_ref.shape
    n = bb * ts
    dmm, dgg = hm_ref.shape[2], hg_ref.shape[2]
    dff = hf_ref.shape[1] if hf_transposed else hf_ref.shape[2]
    mod = mod_ref[...]
    wo_f = wo_ref[dmm:dmm + dff, :]
    mix_f = _dot_tn(hf_ref[0], wo_f) if hf_transposed else _dot(hf_ref[...].reshape(n, dff), wo_f)
    mix = (_dot(hm_ref[...].reshape(n, dmm), wo_ref[0:dmm, :]) + mix_f
           + _dot(hg_ref[...].reshape(n, dgg), wo_ref[dmm + dff:dmm + dff + dgg, :]))
    x1 = _layer_norm(alpha * x_ref[...] + (1.0 + mod[:, 2:3, :]) * mix.reshape(bb, ts, d),
                     l1g_ref[...], l1b_ref[...], LN_EPS)
    h2 = (x1 * (1.0 + mod[:, 4:5, :]) + mod[:, 3:4, :]).reshape(n, d).astype(BF16)
    f = wg_ref.shape[1]
    y = jnp.zeros((n, d), F32)
    for f0 in range(0, f, fchunk):
        gt = _dot(h2, wg_ref[:, f0:f0 + fchunk])
        up = _dot(h2, wu_ref[:, f0:f0 + fchunk])
        act = (gt * _sigmoid(gt) * up).astype(BF16)
        y = y + _dot(act, wd_ref[f0:f0 + fchunk, :])
    o_ref[...] = _layer_norm(alpha * x1 + (1.0 + mod[:, 5:6, :]) * y.reshape(bb, ts, d),
                             l2g_ref[...], l2b_ref[...], LN_EPS)


def _ffn_chunk(f):
    for c in (704, 512, 384, 256, 128):
        if f % c == 0:
            return c
    return f


def _layer_tail(x, mod, hm, hf, hg, wo, l1g, l1b, wg, wu, wd, l2g, l2b, bb, ts, alpha, hf_transposed):
    b, s, d = x.shape
    f = wg.shape[1]
    row = lambda width: pl.BlockSpec((bb, ts, width), lambda i, j: (i, j, 0))
    vec = lambda a: a.reshape(1, d)
    if hf_transposed:
        assert bb == 1
        hf_spec = pl.BlockSpec((1, hf.shape[1], ts), lambda i, j: (i, 0, j))
    else:
        hf_spec = row(hf.shape[2])
    return pl.pallas_call(
        functools.partial(_tail_kernel, alpha=alpha, fchunk=_ffn_chunk(f), hf_transposed=hf_transposed),
        out_shape=jax.ShapeDtypeStruct((b, s, d), F32),
        grid=(b // bb, s // ts),
        in_specs=[row(d), pl.BlockSpec((bb, 6, d), lambda i, j: (i, 0, 0)),
                  row(hm.shape[2]), hf_spec, row(hg.shape[2]),
                  _resident(wo.shape), _resident((1, d)), _resident((1, d)),
                  _resident(wg.shape), _resident(wu.shape), _resident(wd.shape),
                  _resident((1, d)), _resident((1, d))],
        out_specs=row(d),
        compiler_params=_params("parallel", "parallel"),
        name="layer_tail",
    )(x, mod, hm, hf, hg, wo, vec(l1g), vec(l1b), wg, wu, wd, vec(l2g), vec(l2b))


def _tile(b, s):
    if s >= ROW_TILE:
        return 1, ROW_TILE
    bb = max(1, min(b, ROW_TILE // s))
    while b % bb:
        bb -= 1
    return bb, s


def _layer(x, mod, p, cache, dims):
    nh_m, nh_f, ng, hd = dims
    b, s, d = x.shape
    dm, df, dg = nh_m * hd, nh_f * hd, ng * hd
    bb, ts = _tile(b, s)
    alpha = p["alpha"]

    q_scale = hd ** -0.5 * (LOG2E if cache is None else 1.0)
    m_all, fq, fk, fv, fkb, fvb, g_all, gates = _in_projection(
        x, mod, p["w_in"], bb, ts, 4 * dm, df, 2 * dg, q_scale)

    nrow = 2 * nh_m + nh_f
    gates_t = jnp.transpose(gates[:, :, :nrow], (0, 2, 1))
    lm = s if cache is not None else min(MLSTM_PROMPT_CHUNK, s)
    gate_rows = _gate_prepass(gates_t, p["gate_bias"], nh_m, nh_f, lm)
    logf = jnp.transpose(gate_rows[:, 2 * nh_m:2 * nh_m + nh_f, :], (0, 2, 1))
    frow = 2 * nh_m + nh_f

    if cache is None:
        c0 = jnp.zeros((b, nh_m, hd, hd), F32)
        n0 = jnp.zeros((b, nh_m, hd, hd), F32)
        m0 = jnp.full((b, 1, nh_m), -jnp.inf, F32)
    else:
        k_c, v_c, logf_c, c_st, n_st, m_st = cache
        c0 = c_st
        n0 = jnp.broadcast_to(n_st[..., None], (b, nh_m, hd, hd))
        m0 = m_st.reshape(b, 1, nh_m)
    gate_cols = jnp.transpose(gate_rows[:, :2 * nh_m, :], (0, 2, 1))
    hm, c1, n1, m1 = _mlstm(m_all, gate_rows, gate_cols, c0, n0, m0, p["mlstm_norm_g"], nh_m, hd, lm)

    if cache is None:
        f_col = jnp.transpose(gate_rows[:, frow:frow + nh_f, :], (0, 2, 1))
        qa, ka, vt = _fox_prep(fq, fkb, fvb, f_col, nh_f, hd)
        hf = _fox_prompt(qa, ka, vt, nh_f, hd)
    else:
        p_len = k_c.shape[1]
        hf = _fox_sample(fq, fkb, fvb, k_c.reshape(b, p_len, df), v_c.reshape(b, p_len, df),
                         jnp.transpose(logf_c, (0, 2, 1)), gate_rows, nh_f, hd, frow)

    lc = min(s, p["gmlp_ws"].shape[-1])
    hg, vn = _gmlp(g_all, p["gmlp_ws"], p["gmlp_bs_t"], p["gmlp_ln_g"], p["gmlp_ln_b"], bb, ts, ng, hd, lc)

    x_out = _layer_tail(x, mod, hm, hf, hg, p["w_o"], p["ln1_g"], p["ln1_b"], p["w_gate"], p["w_up"],
                        p["w_down"], p["ln2_g"], p["ln2_b"], bb, ts, alpha, cache is None)
    new = (fk.reshape(b, s, nh_f, hd), fv.reshape(b, s, nh_f, hd), logf,
           c1, n1[..., 0], m1.reshape(b, nh_m), vn)
    return x_out, new


def kernel(x_prompt, x_sample, cache_fox_k, cache_fox_v, cache_fox_logf, state_mlstm_C, state_mlstm_n, state_mlstm_m, c_prompt, c_sample, w_ada, b_ada, w_in, b_mlstm_i, b_mlstm_f, mlstm_norm_g, b_fox_f, gmlp_ln_g, gmlp_ln_b, gmlp_ws, gmlp_bs, w_o, ln1_g, ln1_b, w_gate, w_up, w_down, ln2_g, ln2_b):
    depth = w_in.shape[0]
    nh_m = b_mlstm_i.shape[1]
    nh_f = b_fox_f.shape[1]
    ng = gmlp_ws.shape[1]
    hd = cache_fox_k.shape[-1]
    dm, df, dg = nh_m * hd, nh_f * hd, ng * hd
    d = x_prompt.shape[-1]
    bp = x_prompt.shape[0]
    alpha = (2 * depth) ** 0.25

    o_mi = 4 * dm
    o_fq = o_mi + 2 * nh_m
    o_ff = o_fq + 3 * df
    o_gu = o_ff + nh_f
    npad = LANE - (2 * nh_m + nh_f)

    xp, xs = x_prompt, x_sample
    c_all = jnp.concatenate([c_prompt, c_sample], axis=0)
    new_p, new_s = [], []
    for l in range(depth):
        wl = w_in[l]
        w_all = jnp.concatenate(
            [wl[:, :o_mi], wl[:, o_fq:o_ff], wl[:, o_gu:o_gu + 2 * dg], wl[:, o_mi:o_fq], wl[:, o_ff:o_gu],
             jnp.zeros((d, npad), wl.dtype)], axis=1).astype(BF16)
        p = dict(
            alpha=alpha, w_in=w_all,
            gate_bias=jnp.concatenate([b_mlstm_i[l], b_mlstm_f[l], b_fox_f[l]]).reshape(-1, 1),
            mlstm_norm_g=mlstm_norm_g[l], gmlp_ln_g=gmlp_ln_g[l], gmlp_ln_b=gmlp_ln_b[l],
            gmlp_ws=gmlp_ws[l], gmlp_bs_t=jnp.transpose(gmlp_bs[l]),
            w_o=w_o[l].astype(BF16), ln1_g=ln1_g[l], ln1_b=ln1_b[l],
            w_gate=w_gate[l].astype(BF16), w_up=w_up[l].astype(BF16), w_down=w_down[l].astype(BF16),
            ln2_g=ln2_g[l], ln2_b=ln2_b[l])
        mod = _modulation(c_all, w_ada[l], b_ada[l]).reshape(c_all.shape[0], 6, d)
        xp, st_p = _layer(xp, mod[:bp], p, None, (nh_m, nh_f, ng, hd))
        xs, st_s = _layer(xs, mod[bp:], p, (cache_fox_k[l], cache_fox_v[l], cache_fox_logf[l],
                                            state_mlstm_C[l], state_mlstm_n[l], state_mlstm_m[l]),
                          (nh_m, nh_f, ng, hd))
        new_p.append(st_p)
        new_s.append(st_s)

    def stk(lst, i):
        return jnp.stack([e[i] for e in lst], axis=0)

    return (xp, xs,
            stk(new_p, 0), stk(new_p, 1), stk(new_p, 2), stk(new_p, 3), stk(new_p, 4), stk(new_p, 5),
            stk(new_s, 0), stk(new_s, 1), stk(new_s, 2), stk(new_s, 3), stk(new_s, 4), stk(new_s, 5),
            stk(new_s, 6))
```

```python
import functools

import jax
import jax.numpy as jnp
from jax import lax
from jax.experimental import pallas as pl
from jax.experimental.pallas import tpu as pltpu

F32 = jnp.float32
BF16 = jnp.bfloat16

LOG2E = 1.4426950408889634
LN_EPS = 1e-5
HN_EPS = 1e-6
MLSTM_PROMPT_CHUNK = 128
MLSTM_BATCH_ROWS = 4
ROW_TILE = 512
FOX_TILE = 512
FOX_ONES_ROWS = 16
LANE = 128
VMEM_LIMIT = 56 * 1024 * 1024


def _sigmoid(x):
    return 1.0 / (1.0 + jnp.exp(-x))


def _log_sigmoid(x):
    return jnp.minimum(x, 0.0) - jnp.log1p(jnp.exp(-jnp.abs(x)))


def _dot(a, b):
    return jnp.dot(a, b, preferred_element_type=F32)


def _dot_nt(a, b):
    return lax.dot_general(a, b, (((1,), (1,)), ((), ())), preferred_element_type=F32)


def _dot_tn(a, b):
    return lax.dot_general(a, b, (((0,), (0,)), ((), ())), preferred_element_type=F32)


def _dot_f32_by_01(x, u):
    hi = x.astype(BF16)
    r1 = x - hi.astype(F32)
    mid = r1.astype(BF16)
    lo = (r1 - mid.astype(F32)).astype(BF16)
    return _dot(hi, u) + _dot(mid, u) + _dot(lo, u)


def _layer_norm(x, g, b, eps):
    mu = jnp.mean(x, axis=-1, keepdims=True)
    xc = x - mu
    var = jnp.mean(xc * xc, axis=-1, keepdims=True)
    return xc * lax.rsqrt(var + eps) * g + b


def _params(*sem):
    return pltpu.CompilerParams(dimension_semantics=sem, vmem_limit_bytes=VMEM_LIMIT)


def _resident(shape):
    nd = len(shape)
    return pl.BlockSpec(shape, lambda *_: (0,) * nd, pipeline_mode=pl.Buffered(1))


def _mod_kernel(c_ref, w_ref, b_ref, o_ref):
    c = c_ref[...]
    s = (c * _sigmoid(c)).astype(BF16)
    o_ref[...] = _dot(s, w_ref[...].astype(BF16)) + b_ref[...]


def _modulation(c, w_ada, b_ada):
    nb, d = c.shape
    n = w_ada.shape[1]
    tn = n // 4 if (n // 4) % LANE == 0 else n
    return pl.pallas_call(
        _mod_kernel,
        out_shape=jax.ShapeDtypeStruct((nb, n), F32),
        grid=(n // tn,),
        in_specs=[pl.BlockSpec((nb, d), lambda j: (0, 0)),
                  pl.BlockSpec((d, tn), lambda j: (0, j)),
                  pl.BlockSpec((1, tn), lambda j: (0, j))],
        out_specs=pl.BlockSpec((nb, tn), lambda j: (0, j)),
        compiler_params=_params("parallel"),
        name="adaln_mod",
    )(c, w_ada, b_ada.reshape(1, n))


def _modulated(x_ref, mod_ref):
    bb, ts, d = x_ref.shape
    mod = mod_ref[...]
    h = x_ref[...] * (1.0 + mod[:, 1:2, :]) + mod[:, 0:1, :]
    return h.reshape(bb * ts, d).astype(BF16)


def _spatial_gate(gall, gw_ref, gbs_ref, lng_ref, lnb_ref, hg_ref, *, ng, hd, lc):
    bb, ts, dg = hg_ref.shape
    n = bb * ts
    gu = gall[:, :dg]
    vn = _layer_norm(gall[:, dg:], lng_ref[...], lnb_ref[...], LN_EPS)
    row = lax.broadcasted_iota(jnp.int32, (lc, lc), 0)
    col = lax.broadcasted_iota(jnp.int32, (lc, lc), 1)
    tri = col <= row
    per_row = ts // lc
    zs = {}
    for g in range(ng):
        lanes = slice(g * hd, (g + 1) * hd)
        wg = jnp.where(tri, gw_ref[g, 0:lc, 0:lc], 0.0).astype(BF16)
        for ci in range(n // lc):
            zs[g, ci] = _dot(wg, vn[ci * lc:(ci + 1) * lc, lanes].astype(BF16))
    for g in range(ng):
        lanes = slice(g * hd, (g + 1) * hd)
        bcol = gbs_ref[0:lc, g:g + 1]
        for ci in range(n // lc):
            rows = slice(ci * lc, (ci + 1) * lc)
            bi, ri = ci // per_row, ci % per_row
            hg_ref[bi, ri * lc:(ri + 1) * lc, lanes] = (gu[rows, lanes] * (zs[g, ci] + bcol)).astype(BF16)
    return vn


def _inproj_sample_kernel(x_ref, mod_ref, w_ref, gw_ref, gbs_ref, lng_ref, lnb_ref,
                          m_ref, fq_ref, fk_ref, fv_ref, fkb_ref, fvb_ref, hg_ref, vn_ref, gate_ref,
                          *, dm4, df, dg2, q_scale, ng, hd, lc):
    bb, ts, _ = x_ref.shape
    h = _modulated(x_ref, mod_ref)

    def seg(c0, width):
        return _dot(h, w_ref[:, c0:c0 + width])

    m_ref[...] = seg(0, dm4).astype(BF16).reshape(bb, ts, dm4)
    c0 = dm4
    fq_ref[...] = (seg(c0, df) * q_scale).astype(BF16).reshape(bb, ts, df)
    fk = seg(c0 + df, df)
    fk_ref[...] = fk.reshape(bb, ts, df)
    fkb_ref[...] = fk.astype(BF16).reshape(bb, ts, df)
    fv = seg(c0 + 2 * df, df)
    fv_ref[...] = fv.reshape(bb, ts, df)
    fvb_ref[...] = fv.astype(BF16).reshape(bb, ts, df)
    c0 += 3 * df
    gate_ref[...] = seg(c0 + dg2, LANE).reshape(bb, ts, LANE)
    vn = _spatial_gate(seg(c0, dg2), gw_ref, gbs_ref, lng_ref, lnb_ref, hg_ref, ng=ng, hd=hd, lc=lc)
    vn_ref[...] = vn.reshape(bb, ts, dg2 // 2)


def _inproj_prompt_kernel(x_ref, mod_ref, w_ref, gbias_ref, tril_ref, gw_ref, gbs_ref, lng_ref, lnb_ref, *rest,
                          dm4, df, dg2, q_scale, ng, hd, lc, nh, fcol, aliased):
    if aliased:
        rest = rest[2:]
    m_ref, qa_ref, ka_ref, vte_ref, kt_ref, vt_ref, hg_ref, gate_ref, carry = rest
    _, ts, _ = x_ref.shape

    @pl.when(pl.program_id(1) == 0)
    def _():
        carry[...] = jnp.zeros_like(carry)

    h = _modulated(x_ref, mod_ref)

    def seg(c0, width):
        return _dot(h, w_ref[:, c0:c0 + width])

    c_f = dm4
    c_g = dm4 + 3 * df
    gt = seg(c_g + dg2, LANE)
    gate_ref[0] = gt

    logf = _log_sigmoid(gt + gbias_ref[...])
    lower = tril_ref[...]
    hi = logf.astype(BF16)
    r1 = logf - hi.astype(F32)
    mid = r1.astype(BF16)
    lo = (r1 - mid.astype(F32)).astype(BF16)
    fcum = _dot(lower, hi) + _dot(lower, mid) + _dot(lower, lo) + carry[...]
    carry[...] = fcum[ts - 1:ts, :]
    nf = fcum * (-LOG2E)

    fk = seg(c_f + df, df)
    fv = seg(c_f + 2 * df, df)
    kt_ref[0, 0] = jnp.transpose(fk)
    vt = jnp.transpose(fv)
    vt_ref[0, 0] = vt
    if not aliased and kt_ref.shape[0] > 1:
        kt_ref[1:, 0] = jnp.zeros((kt_ref.shape[0] - 1,) + kt_ref.shape[2:], F32)
        vt_ref[1:, 0] = jnp.zeros((vt_ref.shape[0] - 1,) + vt_ref.shape[2:], F32)
    ones = jnp.ones((FOX_ONES_ROWS, ts), F32)
    v_parts = []
    for hh in range(nh):
        v_parts += [vt[hh * hd:(hh + 1) * hd, :], ones]
    vte_ref[0] = jnp.concatenate(v_parts, axis=0).astype(BF16)

    fq = (seg(c_f, df) * q_scale).astype(BF16)
    lane = lax.broadcasted_iota(jnp.int32, (ts, hd), 1)
    ones3 = (lane < 3).astype(BF16)
    fkb = fk.astype(BF16)
    q_parts, k_parts = [], []
    for hh in range(nh):
        lanes = slice(hh * hd, (hh + 1) * hd)
        x = nf[:, fcol + hh:fcol + hh + 1]
        xh = x.astype(BF16).astype(F32)
        x1 = x - xh
        xm = x1.astype(BF16).astype(F32)
        extra = jnp.where(lane == 0, xh, jnp.where(lane == 1, xm, jnp.where(lane == 2, x1 - xm, 0.0)))
        q_parts += [fq[:, lanes], ones3]
        k_parts += [fkb[:, lanes], extra.astype(BF16)]
    gall = seg(c_g, dg2)
    ka_ref[0] = jnp.concatenate(k_parts, axis=-1)
    qa_ref[0] = jnp.concatenate(q_parts, axis=-1)
    m_lo = seg(0, dm4 // 2).astype(BF16)
    _spatial_gate(gall, gw_ref, gbs_ref, lng_ref, lnb_ref, hg_ref, ng=ng, hd=hd, lc=lc)
    m_hi = seg(dm4 // 2, dm4 // 2).astype(BF16)
    m_ref[0, :, 0:dm4 // 2] = m_lo
    m_ref[0, :, dm4 // 2:dm4] = m_hi


def _gmlp_specs(gw, gbs_t, dg):
    return [_resident(gw.shape), _resident(gbs_t.shape), _resident((1, dg)), _resident((1, dg))]


def _in_projection_sample(x, mod, w_all, gm, bb, ts, dm4, df, dg2, q_scale, ng, hd, lc):
    b, s, d = x.shape
    dg = dg2 // 2
    gw, gbs_t, lng, lnb = gm
    row = lambda width: pl.BlockSpec((bb, ts, width), lambda i, j: (i, j, 0))
    shp = lambda width, dt: jax.ShapeDtypeStruct((b, s, width), dt)
    return pl.pallas_call(
        functools.partial(_inproj_sample_kernel, dm4=dm4, df=df, dg2=dg2, q_scale=q_scale, ng=ng, hd=hd, lc=lc),
        out_shape=[shp(dm4, BF16), shp(df, BF16), shp(df, F32), shp(df, F32), shp(df, BF16), shp(df, BF16),
                   shp(dg, BF16), shp(dg, F32), shp(LANE, F32)],
        grid=(b // bb, s // ts),
        in_specs=[row(d), pl.BlockSpec((bb, 6, d), lambda i, j: (i, 0, 0)), _resident(w_all.shape)]
        + _gmlp_specs(gw, gbs_t, dg),
        out_specs=[row(dm4), row(df), row(df), row(df), row(df), row(df), row(dg), row(dg), row(LANE)],
        compiler_params=_params("parallel", "parallel"),
        name="in_proj_sample",
    )(x, mod, w_all, gw, gbs_t, lng.reshape(1, dg), lnb.reshape(1, dg))


def _in_projection_prompt(x, mod, w_all, gate_bias_row, gm, kv_prev, layer, depth, ts, dm4, df, dg2, q_scale,
                          ng, hd, lc, nh, fcol):
    b, s, d = x.shape
    dg = dg2 // 2
    dv = nh * (hd + FOX_ONES_ROWS)
    gw, gbs_t, lng, lnb = gm
    aliased = kv_prev is not None
    row = lambda width: pl.BlockSpec((1, ts, width), lambda i, j: (i, j, 0))
    shp = lambda width, dt: jax.ShapeDtypeStruct((b, s, width), dt)
    kv_shape = jax.ShapeDtypeStruct((depth, b, df, s), F32)
    if aliased:
        kv_spec = pl.BlockSpec((1, 1, df, ts), lambda i, j: (layer, i, 0, j))
    else:
        assert layer == 0
        kv_spec = pl.BlockSpec((depth, 1, df, ts), lambda i, j: (0, i, 0, j))
    in_specs = ([row(d), pl.BlockSpec((1, 6, d), lambda i, j: (i, 0, 0)), _resident(w_all.shape),
                 _resident((1, LANE)), _resident((ts, ts))] + _gmlp_specs(gw, gbs_t, dg))
    tril = jnp.tril(jnp.ones((ts, ts), BF16))
    args = [x, mod, w_all, gate_bias_row, tril, gw, gbs_t, lng.reshape(1, dg), lnb.reshape(1, dg)]
    aliases = {}
    if aliased:
        in_specs += [pl.BlockSpec(memory_space=pl.ANY), pl.BlockSpec(memory_space=pl.ANY)]
        aliases = {len(args): 4, len(args) + 1: 5}
        args += list(kv_prev)
    return pl.pallas_call(
        functools.partial(_inproj_prompt_kernel, dm4=dm4, df=df, dg2=dg2, q_scale=q_scale, ng=ng, hd=hd, lc=lc,
                          nh=nh, fcol=fcol, aliased=aliased),
        out_shape=[shp(dm4, BF16), shp(2 * df, BF16), shp(2 * df, BF16),
                   jax.ShapeDtypeStruct((b, dv, s), BF16), kv_shape, kv_shape, shp(dg, BF16), shp(LANE, F32)],
        grid=(b, s // ts),
        in_specs=in_specs,
        out_specs=[row(dm4), row(2 * df), row(2 * df), pl.BlockSpec((1, dv, ts), lambda i, j: (i, 0, j)),
                   kv_spec, kv_spec, row(dg), row(LANE)],
        scratch_shapes=[pltpu.VMEM((1, LANE), F32)],
        input_output_aliases=aliases,
        compiler_params=_params("parallel", "arbitrary"),
        name="in_proj_prompt",
    )(*args)


def _gates_kernel(g_ref, bias_ref, o_ref, *, hm, hf, lm):
    s = g_ref.shape[2]
    w = min(LANE, s)
    x = g_ref[0] + bias_ref[...]
    ls = _log_sigmoid(x)
    r = lax.broadcasted_iota(jnp.int32, (w, w), 0)
    c = lax.broadcasted_iota(jnp.int32, (w, w), 1)
    upper = r <= c
    u_all = upper.astype(BF16)
    u_chunk = jnp.logical_and(upper, (r // lm) == (c // lm)).astype(BF16)
    off = jnp.zeros((hf, 1), F32)
    for k in range(s // w):
        cols = slice(k * w, (k + 1) * w)
        li = x[0:hm, cols]
        b = _dot_f32_by_01(ls[hm:2 * hm, cols], u_chunk)
        logf = ls[2 * hm:2 * hm + hf, cols]
        fcum = _dot_f32_by_01(logf, u_all) + off
        off = fcum[:, w - 1:w]
        o_ref[0, 0:hm, cols] = li - b
        o_ref[0, hm:2 * hm, cols] = b
        o_ref[0, 2 * hm:2 * hm + hf, cols] = logf
        o_ref[0, 2 * hm + hf:2 * hm + 2 * hf, cols] = fcum


def _gate_prepass(gates_t, bias_col, hm, hf, lm):
    b, nr, s = gates_t.shape
    nout = 2 * hm + 2 * hf
    return pl.pallas_call(
        functools.partial(_gates_kernel, hm=hm, hf=hf, lm=lm),
        out_shape=jax.ShapeDtypeStruct((b, nout, s), F32),
        grid=(b,),
        in_specs=[pl.BlockSpec((1, nr, s), lambda i: (i, 0, 0)), pl.BlockSpec((nr, 1), lambda i: (0, 0))],
        out_specs=pl.BlockSpec((1, nout, s), lambda i: (i, 0, 0)),
        compiler_params=_params("parallel"),
        name="gate_prepass",
    )(gates_t, bias_col)


def _dot_f32_2(x, w):
    hi = x.astype(BF16)
    lo = (x - hi.astype(F32)).astype(BF16)
    return _dot(hi, w) + _dot(lo, w)


def _mlstm_kernel(qkvo_ref, gate_ref, gcol_ref, c0_ref, n0_ref, m0_ref, ng_ref, hm_ref, c_ref, n_ref, m_ref,
                  *, nh, hd, lc):
    @pl.when(pl.program_id(1) == 0)
    def _():
        c_ref[...] = c0_ref[...]
        n_ref[...] = n0_ref[...]
        m_ref[...] = m0_ref[...]

    nb = qkvo_ref.shape[0]
    rows_per_step = qkvo_ref.shape[1]
    dm = nh * hd
    ksc = hd ** -0.5
    row = lax.broadcasted_iota(jnp.int32, (lc, lc), 0)
    col = lax.broadcasted_iota(jnp.int32, (lc, lc), 1)
    tri = col <= row
    neg_inf = jnp.float32(-jnp.inf)
    ones_v = jnp.ones((lc, hd), BF16)
    avg = jnp.full((hd, hd), 1.0 / hd, BF16)

    chains = [(bi, h) for bi in range(nb) for h in range(nh)]
    for ci in range(rows_per_step // lc):
        rows = slice(ci * lc, (ci + 1) * lc)
        m_olds = [m_ref[bi] for bi in range(nb)]
        st = {}
        for bi, h in chains:
            q = qkvo_ref[bi, rows, h * hd:(h + 1) * hd]
            k = qkvo_ref[bi, rows, dm + h * hd:dm + (h + 1) * hd]
            a_row = gate_ref[bi, h:h + 1, rows]
            m0 = m_olds[bi][:, h:h + 1]
            c_old = c_ref[bi, h]
            n_old = n_ref[bi, h]
            a_full = jnp.broadcast_to(a_row, (lc, lc))
            g_col = jnp.maximum(jnp.max(jnp.where(tri, a_full, neg_inf), axis=-1, keepdims=True), m0)
            st[bi, h] = dict(k=k, m0=m0, c_old=c_old, n_old=n_old, a_full=a_full, g_col=g_col,
                             qk=_dot_nt(q, k), qc=_dot(q, c_old.astype(BF16)), qn=_dot(q, n_old.astype(BF16)))
        for bi, h in chains:
            c = st[bi, h]
            v = qkvo_ref[bi, rows, 2 * dm + h * hd:2 * dm + (h + 1) * hd]
            g_b = jnp.broadcast_to(c["g_col"], (lc, max(lc, hd)))
            a_b = jnp.broadcast_to(gcol_ref[bi, rows, h:h + 1], (lc, hd))
            b_b = jnp.broadcast_to(gcol_ref[bi, rows, nh + h:nh + h + 1], (lc, hd))
            g_h = g_b[:, :hd]
            w_intra = jnp.exp(jnp.where(tri, c["a_full"] - g_b[:, :lc], neg_inf))
            c["sc"] = (c["qk"] * (w_intra * ksc)).astype(BF16)
            c["w_inter"] = jnp.exp(c["m0"] - g_h)
            c["floor"] = jnp.exp(-(b_b + g_h))
            c["g_end"] = c["g_col"][lc - 1:lc, :]
            w_end = jnp.exp(a_b - c["g_end"]) * ksc
            c["w_end"] = w_end.astype(BF16)
            c["vw"] = (v.astype(F32) * w_end).astype(BF16)
            c["v"] = v
        m_new = [[] for _ in range(nb)]
        for bi, h in chains:
            c = st[bi, h]
            decay = jnp.exp(c["m0"] - c["g_end"])
            c_ref[bi, h] = decay * c["c_old"] + _dot_tn(c["k"], c["vw"])
            n_ref[bi, h] = decay * c["n_old"] + _dot_tn(c["k"], c["w_end"])
            b_row = gate_ref[bi, nh + h:nh + h + 1, rows]
            m_new[bi].append(b_row[:, lc - 1:lc] + c["g_end"])
        for bi in range(nb):
            m_ref[bi] = jnp.concatenate(m_new[bi], axis=-1)
        for bi, h in chains:
            c = st[bi, h]
            num = _dot(c["sc"], c["v"]) + c["qc"] * c["w_inter"]
            den = _dot(c["sc"], ones_v) + c["qn"] * c["w_inter"]
            c["hh"] = num / jnp.maximum(jnp.abs(den), c["floor"])
        for bi, h in chains:
            c = st[bi, h]
            c["hc"] = c["hh"] - _dot_f32_2(c["hh"], avg)
        for bi, h in chains:
            c = st[bi, h]
            c["var"] = _dot_f32_2(c["hc"] * c["hc"], avg)
        for bi, h in chains:
            c = st[bi, h]
            lanes = slice(h * hd, (h + 1) * hd)
            o = qkvo_ref[bi, rows, 3 * dm + h * hd:3 * dm + (h + 1) * hd]
            out = c["hc"] * lax.rsqrt(c["var"] + HN_EPS) * ng_ref[0:1, lanes] * _sigmoid(o.astype(F32))
            hm_ref[bi, rows, lanes] = out.astype(BF16)


def _mlstm(qkvo, gate_rows, gate_cols, c0, n0_rep, m0, norm_g, nh, hd, lc):
    b, s, d4 = qkvo.shape
    dm = nh * hd
    rows = max(lc, min(LANE, s))
    ngr = gate_rows.shape[1]
    ngc = gate_cols.shape[2]
    nb = MLSTM_BATCH_ROWS if b % MLSTM_BATCH_ROWS == 0 else 1
    state = pl.BlockSpec((nb, nh, hd, hd), lambda i, j: (i, 0, 0, 0))
    mspec = pl.BlockSpec((nb, 1, nh), lambda i, j: (i, 0, 0))
    return pl.pallas_call(
        functools.partial(_mlstm_kernel, nh=nh, hd=hd, lc=lc),
        out_shape=[jax.ShapeDtypeStruct((b, s, dm), BF16),
                   jax.ShapeDtypeStruct((b, nh, hd, hd), F32),
                   jax.ShapeDtypeStruct((b, nh, hd, hd), F32),
                   jax.ShapeDtypeStruct((b, 1, nh), F32)],
        grid=(b // nb, s // rows),
        in_specs=[pl.BlockSpec((nb, rows, d4), lambda i, j: (i, j, 0)),
                  pl.BlockSpec((nb, ngr, rows), lambda i, j: (i, 0, j)),
                  pl.BlockSpec((nb, rows, ngc), lambda i, j: (i, j, 0)),
                  state, state, mspec,
                  pl.BlockSpec((1, dm), lambda i, j: (0, 0))],
        out_specs=[pl.BlockSpec((nb, rows, dm), lambda i, j: (i, j, 0)), state, state, mspec],
        compiler_params=_params("parallel", "arbitrary"),
        name="mlstm_scan",
    )(qkvo, gate_rows, gate_cols, c0, n0_rep, m0, norm_g.reshape(1, dm))


def _fox_prompt_kernel(qa_ref, ka_ref, vt_ref, o_ref, m_sc, acc_sc, *, nh, hd, t):
    i = pl.program_id(1)
    q0 = pl.multiple_of(i * t, t)
    w = 2 * hd
    hv = hd + FOX_ONES_ROWS
    key = lax.broadcasted_iota(jnp.int32, (t, t), 0)
    qry = lax.broadcasted_iota(jnp.int32, (t, t), 1)
    causal = key <= qry
    neg_inf = jnp.float32(-jnp.inf)

    def scores(k0):
        return [_dot_nt(ka_ref[0, pl.ds(k0, t), h * w:(h + 1) * w], qa_ref[0, :, h * w:(h + 1) * w])
                for h in range(nh)]

    def weighted_values(k0, ps):
        return [_dot(vt_ref[0, h * hv:(h + 1) * hv, pl.ds(k0, t)], ps[h]) for h in range(nh)]

    ss = scores(q0)
    ms, ps = [], []
    for h in range(nh):
        s = jnp.where(causal, ss[h], neg_inf)
        m = jnp.max(s, axis=0, keepdims=True)
        ms.append(m)
        ps.append(jnp.exp2(s - m).astype(BF16))
    m_sc[...] = jnp.concatenate(ms, axis=0)
    pvs = weighted_values(q0, ps)
    for h in range(nh):
        acc_sc[h * hv:(h + 1) * hv, :] = pvs[h]

    def body(j, carry):
        k0 = pl.multiple_of(j * t, t)
        ss = scores(k0)
        m_all = m_sc[...]
        ms, ps, alphas = [], [], []
        for h in range(nh):
            m_old = m_all[h:h + 1, :]
            m_new = jnp.maximum(m_old, jnp.max(ss[h], axis=0, keepdims=True))
            alphas.append(jnp.exp2(m_old - m_new))
            ps.append(jnp.exp2(ss[h] - m_new).astype(BF16))
            ms.append(m_new)
        m_sc[...] = jnp.concatenate(ms, axis=0)
        pvs = weighted_values(k0, ps)
        for h in range(nh):
            rows = slice(h * hv, (h + 1) * hv)
            acc_sc[rows, :] = alphas[h] * acc_sc[rows, :] + pvs[h]
        return carry

    lax.fori_loop(0, i, body, 0)
    for h in range(nh):
        o_ref[0, h * hd:(h + 1) * hd, :] = (acc_sc[h * hv:h * hv + hd, :]
                                            / acc_sc[h * hv + hd:h * hv + hd + 1, :]).astype(BF16)


def _fox_prompt(qa, ka, vt, nh, hd):
    b, s, w2 = qa.shape
    df = nh * hd
    dv = vt.shape[1]
    t = min(FOX_TILE, s)
    return pl.pallas_call(
        functools.partial(_fox_prompt_kernel, nh=nh, hd=hd, t=t),
        out_shape=jax.ShapeDtypeStruct((b, df, s), BF16),
        grid=(b, s // t),
        in_specs=[pl.BlockSpec((1, t, w2), lambda i, j: (i, j, 0)),
                  pl.BlockSpec((1, s, w2), lambda i, j: (i, 0, 0)),
                  pl.BlockSpec((1, dv, s), lambda i, j: (i, 0, 0))],
        out_specs=pl.BlockSpec((1, df, t), lambda i, j: (i, 0, j)),
        scratch_shapes=[pltpu.VMEM((nh, t), F32), pltpu.VMEM((dv, t), F32)],
        compiler_params=_params("parallel", "arbitrary"),
        name="fox_prompt",
    )(qa, ka, vt)


def _fox_sample_kernel(q_ref, kn_ref, vn_ref, kc_ref, vc_ref, fc_ref, gate_ref, o_ref, *, nh, hd, frow):
    t = q_ref.shape[1]
    p_len = kc_ref.shape[1]
    w = min(LANE, p_len)
    r = lax.broadcasted_iota(jnp.int32, (w, w), 0)
    c = lax.broadcasted_iota(jnp.int32, (w, w), 1)
    u_all = (r <= c).astype(BF16)
    fc = fc_ref[0]
    parts = []
    off = jnp.zeros((nh, 1), F32)
    for kk in range(p_len // w):
        cs = _dot_f32_by_01(fc[:, kk * w:(kk + 1) * w], u_all) + off
        off = cs[:, w - 1:w]
        parts.append(cs)
    bias_c_all = off - jnp.concatenate(parts, axis=-1)
    bias_n_all = -gate_ref[0, frow:frow + nh, :]
    row = lax.broadcasted_iota(jnp.int32, (t, t), 0)
    col = lax.broadcasted_iota(jnp.int32, (t, t), 1)
    causal = col <= row
    neg_inf = jnp.float32(-jnp.inf)
    for h in range(nh):
        lanes = slice(h * hd, (h + 1) * hd)
        q = q_ref[0, :, lanes]
        s_c = _dot_nt(q, kc_ref[0, :, lanes].astype(BF16)) + bias_c_all[h:h + 1, :]
        s_n = _dot_nt(q, kn_ref[0, :, lanes]) + bias_n_all[h:h + 1, :]
        s_n = jnp.where(causal, s_n, neg_inf)
        m = jnp.maximum(jnp.max(s_c, axis=-1, keepdims=True), jnp.max(s_n, axis=-1, keepdims=True))
        p_c = jnp.exp(s_c - m)
        p_n = jnp.exp(s_n - m)
        l = jnp.sum(p_c, axis=-1, keepdims=True) + jnp.sum(p_n, axis=-1, keepdims=True)
        acc = _dot(p_c.astype(BF16), vc_ref[0, :, lanes].astype(BF16)) + _dot(p_n.astype(BF16), vn_ref[0, :, lanes])
        o_ref[0, :, lanes] = (acc / l).astype(BF16)


def _fox_sample(q, k_new, v_new, k_cache, v_cache, logf_cache_t, gate_rows, nh, hd, frow):
    b, t, df = q.shape
    p_len = k_cache.shape[1]
    ngr = gate_rows.shape[1]
    blk = lambda n, width: pl.BlockSpec((1, n, width), lambda i: (i, 0, 0))
    return pl.pallas_call(
        functools.partial(_fox_sample_kernel, nh=nh, hd=hd, frow=frow),
        out_shape=jax.ShapeDtypeStruct((b, t, df), BF16),
        grid=(b,),
        in_specs=[blk(t, df), blk(t, df), blk(t, df), blk(p_len, df), blk(p_len, df), blk(nh, p_len), blk(ngr, t)],
        out_specs=blk(t, df),
        compiler_params=_params("parallel"),
        name="fox_sample",
    )(q, k_new, v_new, k_cache, v_cache, logf_cache_t, gate_rows)


def _tail_kernel(x_ref, mod_ref, hm_ref, hf_ref, hg_ref, wo_ref, l1g_ref, l1b_ref, wg_ref, wu_ref, wd_ref,
                 l2g_ref, l2b_ref, o_ref, *, alpha, fchunk, hf_transposed):
    bb, ts, d = x_ref.shape
    n = bb * ts
    dmm, dgg = hm_ref.shape[2], hg_ref.shape[2]
    dff = hf_ref.shape[1] if hf_transposed else hf_ref.shape[2]
    mod = mod_ref[...]
    wo_f = wo_ref[dmm:dmm + dff, :]
    mix_f = _dot_tn(hf_ref[0], wo_f) if hf_transposed else _dot(hf_ref[...].reshape(n, dff), wo_f)
    mix = (_dot(hm_ref[...].reshape(n, dmm), wo_ref[0:dmm, :]) + mix_f
           + _dot(hg_ref[...].reshape(n, dgg), wo_ref[dmm + dff:dmm + dff + dgg, :]))
    x1 = _layer_norm(alpha * x_ref[...] + (1.0 + mod[:, 2:3, :]) * mix.reshape(bb, ts, d),
                     l1g_ref[...], l1b_ref[...], LN_EPS)
    h2 = (x1 * (1.0 + mod[:, 4:5, :]) + mod[:, 3:4, :]).reshape(n, d).astype(BF16)
    f = wg_ref.shape[1]
    y = jnp.zeros((n, d), F32)
    for f0 in range(0, f, fchunk):
        gt = _dot(h2, wg_ref[:, f0:f0 + fchunk])
        up = _dot(h2, wu_ref[:, f0:f0 + fchunk])
        act = (gt * _sigmoid(gt) * up).astype(BF16)
        y = y + _dot(act, wd_ref[f0:f0 + fchunk, :])
    o_ref[...] = _layer_norm(alpha * x1 + (1.0 + mod[:, 5:6, :]) * y.reshape(bb, ts, d),
                             l2g_ref[...], l2b_ref[...], LN_EPS)


def _ffn_chunk(f):
    for c in (704, 512, 384, 256, 128):
        if f % c == 0:
            return c
    return f


def _layer_tail(x, mod, hm, hf, hg, wo, l1g, l1b, wg, wu, wd, l2g, l2b, bb, ts, alpha, hf_transposed):
    b, s, d = x.shape
    f = wg.shape[1]
    row = lambda width: pl.BlockSpec((bb, ts, width), lambda i, j: (i, j, 0))
    vec = lambda a: a.reshape(1, d)
    if hf_transposed:
        assert bb == 1
        hf_spec = pl.BlockSpec((1, hf.shape[1], ts), lambda i, j: (i, 0, j))
    else:
        hf_spec = row(hf.shape[2])
    return pl.pallas_call(
        functools.partial(_tail_kernel, alpha=alpha, fchunk=_ffn_chunk(f), hf_transposed=hf_transposed),
        out_shape=jax.ShapeDtypeStruct((b, s, d), F32),
        grid=(b // bb, s // ts),
        in_specs=[row(d), pl.BlockSpec((bb, 6, d), lambda i, j: (i, 0, 0)),
                  row(hm.shape[2]), hf_spec, row(hg.shape[2]),
                  _resident(wo.shape), _resident((1, d)), _resident((1, d)),
                  _resident(wg.shape), _resident(wu.shape), _resident(wd.shape),
                  _resident((1, d)), _resident((1, d))],
        out_specs=row(d),
        compiler_params=_params("parallel", "parallel"),
        name="layer_tail",
    )(x, mod, hm, hf, hg, wo, vec(l1g), vec(l1b), wg, wu, wd, vec(l2g), vec(l2b))


def _tile(b, s):
    if s >= ROW_TILE:
        return 1, ROW_TILE
    bb = max(1, min(b, ROW_TILE // s))
    while b % bb:
        bb -= 1
    return bb, s


def _layer(x, mod, p, cache, dims, kv_prev=None):
    nh_m, nh_f, ng, hd = dims
    b, s, d = x.shape
    dm, df, dg = nh_m * hd, nh_f * hd, ng * hd
    bb, ts = _tile(b, s)
    alpha = p["alpha"]
    lc = min(s, p["gmlp_ws"].shape[-1])
    gm = (p["gmlp_ws"], p["gmlp_bs_t"], p["gmlp_ln_g"], p["gmlp_ln_b"])
    nrow = 2 * nh_m + nh_f

    if cache is None:
        gate_bias_row = jnp.pad(p["gate_bias"].reshape(1, nrow), ((0, 0), (0, LANE - nrow)))
        m_all, qa, ka, vte, k_t, v_t, hg, gates = _in_projection_prompt(
            x, mod, p["w_in"], gate_bias_row, gm, kv_prev, p["layer"], p["depth"], ts, 4 * dm, df, 2 * dg,
            hd ** -0.5 * LOG2E, ng, hd, lc, nh_f, 2 * nh_m)
    else:
        m_all, fq, fk, fv, fkb, fvb, hg, vn, gates = _in_projection_sample(
            x, mod, p["w_in"], gm, bb, ts, 4 * dm, df, 2 * dg, hd ** -0.5, ng, hd, lc)

    gates_t = jnp.transpose(gates[:, :, :nrow], (0, 2, 1))
    lm = s if cache is not None else min(MLSTM_PROMPT_CHUNK, s)
    gate_rows = _gate_prepass(gates_t, p["gate_bias"], nh_m, nh_f, lm)
    logf = jnp.transpose(gate_rows[:, 2 * nh_m:2 * nh_m + nh_f, :], (0, 2, 1))
    frow = 2 * nh_m + nh_f

    if cache is None:
        c0 = jnp.zeros((b, nh_m, hd, hd), F32)
        n0 = jnp.zeros((b, nh_m, hd, hd), F32)
        m0 = jnp.full((b, 1, nh_m), -jnp.inf, F32)
    else:
        k_c, v_c, logf_c, c_st, n_st, m_st = cache
        c0 = c_st
        n0 = jnp.broadcast_to(n_st[..., None], (b, nh_m, hd, hd))
        m0 = m_st.reshape(b, 1, nh_m)
    gate_cols = jnp.transpose(gate_rows[:, :2 * nh_m, :], (0, 2, 1))
    hm, c1, n1, m1 = _mlstm(m_all, gate_rows, gate_cols, c0, n0, m0, p["mlstm_norm_g"], nh_m, hd, lm)

    if cache is None:
        hf = _fox_prompt(qa, ka, vte, nh_f, hd)
    else:
        p_len = k_c.shape[1]
        hf = _fox_sample(fq, fkb, fvb, k_c.reshape(b, p_len, df), v_c.reshape(b, p_len, df),
                         jnp.transpose(logf_c, (0, 2, 1)), gate_rows, nh_f, hd, frow)

    x_out = _layer_tail(x, mod, hm, hf, hg, p["w_o"], p["ln1_g"], p["ln1_b"], p["w_gate"], p["w_up"],
                        p["w_down"], p["ln2_g"], p["ln2_b"], bb, ts, alpha, cache is None)
    if cache is None:
        new = (k_t, v_t, logf, c1, n1[..., 0], m1.reshape(b, nh_m))
    else:
        new = (fk.reshape(b, s, nh_f, hd), fv.reshape(b, s, nh_f, hd), logf,
               c1, n1[..., 0], m1.reshape(b, nh_m), vn)
    return x_out, new


def kernel(x_prompt, x_sample, cache_fox_k, cache_fox_v, cache_fox_logf, state_mlstm_C, state_mlstm_n, state_mlstm_m, c_prompt, c_sample, w_ada, b_ada, w_in, b_mlstm_i, b_mlstm_f, mlstm_norm_g, b_fox_f, gmlp_ln_g, gmlp_ln_b, gmlp_ws, gmlp_bs, w_o, ln1_g, ln1_b, w_gate, w_up, w_down, ln2_g, ln2_b):
    depth = w_in.shape[0]
    nh_m = b_mlstm_i.shape[1]
    nh_f = b_fox_f.shape[1]
    ng = gmlp_ws.shape[1]
    hd = cache_fox_k.shape[-1]
    dm, df, dg = nh_m * hd, nh_f * hd, ng * hd
    d = x_prompt.shape[-1]
    bp = x_prompt.shape[0]
    alpha = (2 * depth) ** 0.25

    o_mi = 4 * dm
    o_fq = o_mi + 2 * nh_m
    o_ff = o_fq + 3 * df
    o_gu = o_ff + nh_f
    npad = LANE - (2 * nh_m + nh_f)

    xp, xs = x_prompt, x_sample
    c_all = jnp.concatenate([c_prompt, c_sample], axis=0)
    new_p, new_s = [], []
    kv_prev = None
    for l in range(depth):
        wl = w_in[l]
        w_all = jnp.concatenate(
            [wl[:, :o_mi], wl[:, o_fq:o_ff], wl[:, o_gu:o_gu + 2 * dg], wl[:, o_mi:o_fq], wl[:, o_ff:o_gu],
             jnp.zeros((d, npad), wl.dtype)], axis=1).astype(BF16)
        p = dict(
            alpha=alpha, w_in=w_all, layer=l, depth=depth,
            gate_bias=jnp.concatenate([b_mlstm_i[l], b_mlstm_f[l], b_fox_f[l]]).reshape(-1, 1),
            mlstm_norm_g=mlstm_norm_g[l], gmlp_ln_g=gmlp_ln_g[l], gmlp_ln_b=gmlp_ln_b[l],
            gmlp_ws=gmlp_ws[l], gmlp_bs_t=jnp.transpose(gmlp_bs[l]),
            w_o=w_o[l].astype(BF16), ln1_g=ln1_g[l], ln1_b=ln1_b[l],
            w_gate=w_gate[l].astype(BF16), w_up=w_up[l].astype(BF16), w_down=w_down[l].astype(BF16),
            ln2_g=ln2_g[l], ln2_b=ln2_b[l])
        mod = _modulation(c_all, w_ada[l], b_ada[l]).reshape(c_all.shape[0], 6, d)
        xp, st_p = _layer(xp, mod[:bp], p, None, (nh_m, nh_f, ng, hd), kv_prev)
        kv_prev = st_p[:2]
        xs, st_s = _layer(xs, mod[bp:], p, (cache_fox_k[l], cache_fox_v[l], cache_fox_logf[l],
                                            state_mlstm_C[l], state_mlstm_n[l], state_mlstm_m[l]),
                          (nh_m, nh_f, ng, hd))
        new_p.append(st_p)
        new_s.append(st_s)

    def stk(lst, i):
        return jnp.stack([e[i] for e in lst], axis=0)

    def heads_last(kv_t):
        sp = kv_t.shape[-1]
        return jnp.transpose(kv_t.reshape(depth, bp, nh_f, hd, sp), (0, 1, 4, 2, 3))

    return (xp, xs,
            heads_last(kv_prev[0]), heads_last(kv_prev[1]), stk(new_p, 2), stk(new_p, 3), stk(new_p, 4), stk(new_p, 5),
            stk(new_s, 0), stk(new_s, 1), stk(new_s, 2), stk(new_s, 3), stk(new_s, 4), stk(new_s, 5),
            stk(new_s, 6))
```

```python
import functools

import jax
import jax.numpy as jnp
from jax import lax
from jax.experimental import pallas as pl
from jax.experimental.pallas import tpu as pltpu

F32 = jnp.float32
BF16 = jnp.bfloat16

LOG2E = 1.4426950408889634
LN_EPS = 1e-5
HN_EPS = 1e-6
MLSTM_PROMPT_CHUNK = 256
MLSTM_BATCH_ROWS = 4
ROW_TILE = 512
TAIL_SPLIT = 2
FOX_TILE = 512
FOX_ONES_ROWS = 16
LANE = 128
VMEM_LIMIT = 56 * 1024 * 1024


def _sigmoid(x):
    return 1.0 / (1.0 + jnp.exp(-x))


def _log_sigmoid(x):
    return jnp.minimum(x, 0.0) - jnp.log1p(jnp.exp(-jnp.abs(x)))


def _dot(a, b):
    return jnp.dot(a, b, preferred_element_type=F32)


def _dot_nt(a, b):
    return lax.dot_general(a, b, (((1,), (1,)), ((), ())), preferred_element_type=F32)


def _dot_tn(a, b):
    return lax.dot_general(a, b, (((0,), (0,)), ((), ())), preferred_element_type=F32)


def _dot_f32_by_01(x, u):
    hi = x.astype(BF16)
    r1 = x - hi.astype(F32)
    mid = r1.astype(BF16)
    lo = (r1 - mid.astype(F32)).astype(BF16)
    return _dot(hi, u) + _dot(mid, u) + _dot(lo, u)


def _layer_norm(x, g, b, eps):
    mu = jnp.mean(x, axis=-1, keepdims=True)
    xc = x - mu
    var = jnp.mean(xc * xc, axis=-1, keepdims=True)
    return xc * lax.rsqrt(var + eps) * g + b


def _params(*sem):
    return pltpu.CompilerParams(dimension_semantics=sem, vmem_limit_bytes=VMEM_LIMIT)


def _resident(shape):
    nd = len(shape)
    return pl.BlockSpec(shape, lambda *_: (0,) * nd, pipeline_mode=pl.Buffered(1))


def _mod_kernel(c_ref, w_ref, b_ref, o_ref):
    c = c_ref[...]
    s = (c * _sigmoid(c)).astype(BF16)
    o_ref[...] = _dot(s, w_ref[0].astype(BF16)) + b_ref[0]


def _modulation(c, w_ada, b_ada, layer):
    nb, d = c.shape
    depth, _, n = w_ada.shape
    tn = n // 4 if (n // 4) % LANE == 0 else n
    return pl.pallas_call(
        _mod_kernel,
        out_shape=jax.ShapeDtypeStruct((nb, n), F32),
        grid=(n // tn,),
        in_specs=[pl.BlockSpec((nb, d), lambda j: (0, 0)),
                  pl.BlockSpec((1, d, tn), lambda j: (layer, 0, j)),
                  pl.BlockSpec((1, 1, tn), lambda j: (layer, 0, j))],
        out_specs=pl.BlockSpec((nb, tn), lambda j: (0, j)),
        compiler_params=_params("parallel"),
        name="adaln_mod",
    )(c, w_ada, b_ada.reshape(depth, 1, n))


def _modulated(x_ref, mod_ref):
    bb, ts, d = x_ref.shape
    mod = mod_ref[...]
    h = x_ref[...] * (1.0 + mod[:, 1:2, :]) + mod[:, 0:1, :]
    return h.reshape(bb * ts, d).astype(BF16)


def _spatial_gate(gall, gw_ref, gbs_ref, lng_ref, lnb_ref, hg_ref, *, ng, hd, lc):
    bb, ts, dg = hg_ref.shape
    n = bb * ts
    gu = gall[:, :dg]
    vn = _layer_norm(gall[:, dg:], lng_ref[...], lnb_ref[...], LN_EPS)
    row = lax.broadcasted_iota(jnp.int32, (lc, lc), 0)
    col = lax.broadcasted_iota(jnp.int32, (lc, lc), 1)
    tri = col <= row
    per_row = ts // lc
    zs = {}
    for g in range(ng):
        lanes = slice(g * hd, (g + 1) * hd)
        wg = jnp.where(tri, gw_ref[g, 0:lc, 0:lc], 0.0).astype(BF16)
        for ci in range(n // lc):
            zs[g, ci] = _dot(wg, vn[ci * lc:(ci + 1) * lc, lanes].astype(BF16))
    for g in range(ng):
        lanes = slice(g * hd, (g + 1) * hd)
        bcol = gbs_ref[0:lc, g:g + 1]
        for ci in range(n // lc):
            rows = slice(ci * lc, (ci + 1) * lc)
            bi, ri = ci // per_row, ci % per_row
            hg_ref[bi, ri * lc:(ri + 1) * lc, lanes] = (gu[rows, lanes] * (zs[g, ci] + bcol)).astype(BF16)
    return vn


def _inproj_sample_kernel(x_ref, mod_ref, w_ref, gw_ref, gbs_ref, lng_ref, lnb_ref,
                          m_ref, fq_ref, fk_ref, fv_ref, fkb_ref, fvb_ref, hg_ref, vn_ref, gate_ref,
                          *, dm4, df, dg2, q_scale, ng, hd, lc):
    bb, ts, _ = x_ref.shape
    h = _modulated(x_ref, mod_ref)

    def seg(c0, width):
        return _dot(h, w_ref[:, c0:c0 + width])

    m_ref[...] = seg(0, dm4).astype(BF16).reshape(bb, ts, dm4)
    c0 = dm4
    fq_ref[...] = (seg(c0, df) * q_scale).astype(BF16).reshape(bb, ts, df)
    fk = seg(c0 + df, df)
    fk_ref[...] = fk.reshape(bb, ts, df)
    fkb_ref[...] = fk.astype(BF16).reshape(bb, ts, df)
    fv = seg(c0 + 2 * df, df)
    fv_ref[...] = fv.reshape(bb, ts, df)
    fvb_ref[...] = fv.astype(BF16).reshape(bb, ts, df)
    c0 += 3 * df
    gate_ref[...] = seg(c0 + dg2, LANE).reshape(bb, ts, LANE)
    vn = _spatial_gate(seg(c0, dg2), gw_ref, gbs_ref, lng_ref, lnb_ref, hg_ref, ng=ng, hd=hd, lc=lc)
    vn_ref[...] = vn.reshape(bb, ts, dg2 // 2)


def _inproj_prompt_kernel(x_ref, mod_ref, w_ref, gbias_ref, tril_ref, gw_ref, gbs_ref, lng_ref, lnb_ref, *rest,
                          dm4, df, dg2, q_scale, ng, hd, lc, nh, fcol, aliased):
    if aliased:
        rest = rest[2:]
    m_ref, qa_ref, ka_ref, vte_ref, kt_ref, vt_ref, hg_ref, gate_ref, carry = rest
    _, ts, _ = x_ref.shape

    @pl.when(pl.program_id(1) == 0)
    def _():
        carry[...] = jnp.zeros_like(carry)

    h = _modulated(x_ref, mod_ref)

    def seg(c0, width):
        return _dot(h, w_ref[:, c0:c0 + width])

    c_f = dm4
    c_g = dm4 + 3 * df
    gt = seg(c_g + dg2, LANE)
    gate_ref[0] = gt

    logf = _log_sigmoid(gt + gbias_ref[...])
    lower = tril_ref[...]
    hi = logf.astype(BF16)
    r1 = logf - hi.astype(F32)
    mid = r1.astype(BF16)
    lo = (r1 - mid.astype(F32)).astype(BF16)
    fcum = _dot(lower, hi) + _dot(lower, mid) + _dot(lower, lo) + carry[...]
    carry[...] = fcum[ts - 1:ts, :]
    nf = fcum * (-LOG2E)

    fk = seg(c_f + df, df)
    fv = seg(c_f + 2 * df, df)
    kt_ref[0, 0] = jnp.transpose(fk)
    vt = jnp.transpose(fv)
    vt_ref[0, 0] = vt
    if not aliased and kt_ref.shape[0] > 1:
        kt_ref[1:, 0] = jnp.zeros((kt_ref.shape[0] - 1,) + kt_ref.shape[2:], F32)
        vt_ref[1:, 0] = jnp.zeros((vt_ref.shape[0] - 1,) + vt_ref.shape[2:], F32)
    ones = jnp.ones((FOX_ONES_ROWS, ts), F32)
    v_parts = []
    for hh in range(nh):
        v_parts += [vt[hh * hd:(hh + 1) * hd, :], ones]
    vte_ref[0] = jnp.concatenate(v_parts, axis=0).astype(BF16)

    fq = (seg(c_f, df) * q_scale).astype(BF16)
    lane = lax.broadcasted_iota(jnp.int32, (ts, hd), 1)
    ones3 = (lane < 3).astype(BF16)
    fkb = fk.astype(BF16)
    q_parts, k_parts = [], []
    for hh in range(nh):
        lanes = slice(hh * hd, (hh + 1) * hd)
        x = nf[:, fcol + hh:fcol + hh + 1]
        xh = x.astype(BF16).astype(F32)
        x1 = x - xh
        xm = x1.astype(BF16).astype(F32)
        extra = jnp.where(lane == 0, xh, jnp.where(lane == 1, xm, jnp.where(lane == 2, x1 - xm, 0.0)))
        q_parts += [fq[:, lanes], ones3]
        k_parts += [fkb[:, lanes], extra.astype(BF16)]
    gall = seg(c_g, dg2)
    ka_ref[0] = jnp.concatenate(k_parts, axis=-1)
    qa_ref[0] = jnp.concatenate(q_parts, axis=-1)
    m_lo = seg(0, dm4 // 2).astype(BF16)
    _spatial_gate(gall, gw_ref, gbs_ref, lng_ref, lnb_ref, hg_ref, ng=ng, hd=hd, lc=lc)
    m_hi = seg(dm4 // 2, dm4 // 2).astype(BF16)
    m_ref[0, :, 0:dm4 // 2] = m_lo
    m_ref[0, :, dm4 // 2:dm4] = m_hi


def _gmlp_specs(gw, gbs_t, dg):
    return [_resident(gw.shape), _resident(gbs_t.shape), _resident((1, dg)), _resident((1, dg))]


def _in_projection_sample(x, mod, w_all, gm, bb, ts, dm4, df, dg2, q_scale, ng, hd, lc):
    b, s, d = x.shape
    dg = dg2 // 2
    gw, gbs_t, lng, lnb = gm
    row = lambda width: pl.BlockSpec((bb, ts, width), lambda i, j: (i, j, 0))
    shp = lambda width, dt: jax.ShapeDtypeStruct((b, s, width), dt)
    return pl.pallas_call(
        functools.partial(_inproj_sample_kernel, dm4=dm4, df=df, dg2=dg2, q_scale=q_scale, ng=ng, hd=hd, lc=lc),
        out_shape=[shp(dm4, BF16), shp(df, BF16), shp(df, F32), shp(df, F32), shp(df, BF16), shp(df, BF16),
                   shp(dg, BF16), shp(dg, F32), shp(LANE, F32)],
        grid=(b // bb, s // ts),
        in_specs=[row(d), pl.BlockSpec((bb, 6, d), lambda i, j: (i, 0, 0)), _resident(w_all.shape)]
        + _gmlp_specs(gw, gbs_t, dg),
        out_specs=[row(dm4), row(df), row(df), row(df), row(df), row(df), row(dg), row(dg), row(LANE)],
        compiler_params=_params("parallel", "parallel"),
        name="in_proj_sample",
    )(x, mod, w_all, gw, gbs_t, lng.reshape(1, dg), lnb.reshape(1, dg))


def _in_projection_prompt(x, mod, w_all, gate_bias_row, gm, kv_prev, layer, depth, ts, dm4, df, dg2, q_scale,
                          ng, hd, lc, nh, fcol):
    b, s, d = x.shape
    dg = dg2 // 2
    dv = nh * (hd + FOX_ONES_ROWS)
    gw, gbs_t, lng, lnb = gm
    aliased = kv_prev is not None
    row = lambda width: pl.BlockSpec((1, ts, width), lambda i, j: (i, j, 0))
    shp = lambda width, dt: jax.ShapeDtypeStruct((b, s, width), dt)
    kv_shape = jax.ShapeDtypeStruct((depth, b, df, s), F32)
    if aliased:
        kv_spec = pl.BlockSpec((1, 1, df, ts), lambda i, j: (layer, i, 0, j))
    else:
        assert layer == 0
        kv_spec = pl.BlockSpec((depth, 1, df, ts), lambda i, j: (0, i, 0, j))
    in_specs = ([row(d), pl.BlockSpec((1, 6, d), lambda i, j: (i, 0, 0)), _resident(w_all.shape),
                 _resident((1, LANE)), _resident((ts, ts))] + _gmlp_specs(gw, gbs_t, dg))
    tril = jnp.tril(jnp.ones((ts, ts), BF16))
    args = [x, mod, w_all, gate_bias_row, tril, gw, gbs_t, lng.reshape(1, dg), lnb.reshape(1, dg)]
    aliases = {}
    if aliased:
        in_specs += [pl.BlockSpec(memory_space=pl.ANY), pl.BlockSpec(memory_space=pl.ANY)]
        aliases = {len(args): 4, len(args) + 1: 5}
        args += list(kv_prev)
    return pl.pallas_call(
        functools.partial(_inproj_prompt_kernel, dm4=dm4, df=df, dg2=dg2, q_scale=q_scale, ng=ng, hd=hd, lc=lc,
                          nh=nh, fcol=fcol, aliased=aliased),
        out_shape=[shp(dm4, BF16), shp(2 * df, BF16), shp(2 * df, BF16),
                   jax.ShapeDtypeStruct((b, dv, s), BF16), kv_shape, kv_shape, shp(dg, BF16), shp(LANE, F32)],
        grid=(b, s // ts),
        in_specs=in_specs,
        out_specs=[row(dm4), row(2 * df), row(2 * df), pl.BlockSpec((1, dv, ts), lambda i, j: (i, 0, j)),
                   kv_spec, kv_spec, row(dg), row(LANE)],
        scratch_shapes=[pltpu.VMEM((1, LANE), F32)],
        input_output_aliases=aliases,
        compiler_params=_params("parallel", "arbitrary"),
        name="in_proj_prompt",
    )(*args)


def _gates_kernel(g_ref, bias_ref, o_ref, *, hm, hf, lm):
    s = g_ref.shape[2]
    w = min(LANE, s)
    x = g_ref[0] + bias_ref[...]
    ls = _log_sigmoid(x)
    r = lax.broadcasted_iota(jnp.int32, (w, w), 0)
    c = lax.broadcasted_iota(jnp.int32, (w, w), 1)
    upper = r <= c
    u_all = upper.astype(BF16)
    u_chunk = jnp.logical_and(upper, (r // lm) == (c // lm)).astype(BF16)
    off = jnp.zeros((hf, 1), F32)
    b_off = jnp.zeros((hm, 1), F32)
    for k in range(s // w):
        cols = slice(k * w, (k + 1) * w)
        li = x[0:hm, cols]
        if (k * w) % lm == 0:
            b_off = jnp.zeros((hm, 1), F32)
        b = _dot_f32_by_01(ls[hm:2 * hm, cols], u_chunk) + b_off
        if lm > w:
            b_off = b[:, w - 1:w]
        logf = ls[2 * hm:2 * hm + hf, cols]
        fcum = _dot_f32_by_01(logf, u_all) + off
        off = fcum[:, w - 1:w]
        o_ref[0, 0:hm, cols] = li - b
        o_ref[0, hm:2 * hm, cols] = b
        o_ref[0, 2 * hm:2 * hm + hf, cols] = logf
        o_ref[0, 2 * hm + hf:2 * hm + 2 * hf, cols] = fcum


def _gate_prepass(gates_t, bias_col, hm, hf, lm):
    b, nr, s = gates_t.shape
    nout = 2 * hm + 2 * hf
    return pl.pallas_call(
        functools.partial(_gates_kernel, hm=hm, hf=hf, lm=lm),
        out_shape=jax.ShapeDtypeStruct((b, nout, s), F32),
        grid=(b,),
        in_specs=[pl.BlockSpec((1, nr, s), lambda i: (i, 0, 0)), pl.BlockSpec((nr, 1), lambda i: (0, 0))],
        out_specs=pl.BlockSpec((1, nout, s), lambda i: (i, 0, 0)),
        compiler_params=_params("parallel"),
        name="gate_prepass",
    )(gates_t, bias_col)


def _dot_f32_2(x, w):
    hi = x.astype(BF16)
    lo = (x - hi.astype(F32)).astype(BF16)
    return _dot(hi, w) + _dot(lo, w)


def _mlstm_kernel(qkvo_ref, gate_ref, gcol_ref, c0_ref, n0_ref, m0_ref, ng_ref, hm_ref, c_ref, n_ref, m_ref,
                  *, nh, hd, lc):
    @pl.when(pl.program_id(1) == 0)
    def _():
        c_ref[...] = c0_ref[...]
        n_ref[...] = n0_ref[...]
        m_ref[...] = m0_ref[...]

    nb = qkvo_ref.shape[0]
    rows_per_step = qkvo_ref.shape[1]
    dm = nh * hd
    ksc = hd ** -0.5
    row = lax.broadcasted_iota(jnp.int32, (lc, lc), 0)
    col = lax.broadcasted_iota(jnp.int32, (lc, lc), 1)
    tri = col <= row
    neg_inf = jnp.float32(-jnp.inf)
    ones_v = jnp.ones((lc, hd), BF16)
    avg = jnp.full((hd, hd), 1.0 / hd, BF16)

    chains = [(bi, h) for bi in range(nb) for h in range(nh)]
    for ci in range(rows_per_step // lc):
        rows = slice(ci * lc, (ci + 1) * lc)
        m_olds = [m_ref[bi] for bi in range(nb)]
        st = {}
        for bi, h in chains:
            q = qkvo_ref[bi, rows, h * hd:(h + 1) * hd]
            k = qkvo_ref[bi, rows, dm + h * hd:dm + (h + 1) * hd]
            a_row = gate_ref[bi, h:h + 1, rows]
            m0 = m_olds[bi][:, h:h + 1]
            c_old = c_ref[bi, h]
            n_old = n_ref[bi, h]
            a_full = jnp.broadcast_to(a_row, (lc, lc))
            g_col = jnp.maximum(jnp.max(jnp.where(tri, a_full, neg_inf), axis=-1, keepdims=True), m0)
            st[bi, h] = dict(k=k, m0=m0, c_old=c_old, n_old=n_old, a_full=a_full, g_col=g_col,
                             qk=_dot_nt(q, k), qc=_dot(q, c_old.astype(BF16)), qn=_dot(q, n_old.astype(BF16)))
        for bi, h in chains:
            c = st[bi, h]
            v = qkvo_ref[bi, rows, 2 * dm + h * hd:2 * dm + (h + 1) * hd]
            g_b = jnp.broadcast_to(c["g_col"], (lc, max(lc, hd)))
            a_b = jnp.broadcast_to(gcol_ref[bi, rows, h:h + 1], (lc, hd))
            b_b = jnp.broadcast_to(gcol_ref[bi, rows, nh + h:nh + h + 1], (lc, hd))
            g_h = g_b[:, :hd]
            w_intra = jnp.exp(jnp.where(tri, c["a_full"] - g_b[:, :lc], neg_inf))
            c["sc"] = (c["qk"] * (w_intra * ksc)).astype(BF16)
            c["w_inter"] = jnp.exp(c["m0"] - g_h)
            c["floor"] = jnp.exp(-(b_b + g_h))
            c["g_end"] = c["g_col"][lc - 1:lc, :]
            w_end = jnp.exp(a_b - c["g_end"]) * ksc
            c["w_end"] = w_end.astype(BF16)
            c["vw"] = v * c["w_end"]
            c["v"] = v
        m_new = [[] for _ in range(nb)]
        for bi, h in chains:
            c = st[bi, h]
            decay = jnp.exp(c["m0"] - c["g_end"])
            c_ref[bi, h] = decay * c["c_old"] + _dot_tn(c["k"], c["vw"])
            n_ref[bi, h] = decay * c["n_old"] + _dot_tn(c["k"], c["w_end"])
            b_row = gate_ref[bi, nh + h:nh + h + 1, rows]
            m_new[bi].append(b_row[:, lc - 1:lc] + c["g_end"])
        for bi in range(nb):
            m_ref[bi] = jnp.concatenate(m_new[bi], axis=-1)
        for bi, h in chains:
            c = st[bi, h]
            num = _dot(c["sc"], c["v"]) + c["qc"] * c["w_inter"]
            den = _dot(c["sc"], ones_v) + c["qn"] * c["w_inter"]
            c["hh"] = num / jnp.maximum(jnp.abs(den), c["floor"])
        for bi, h in chains:
            c = st[bi, h]
            c["hc"] = c["hh"] - _dot(c["hh"].astype(BF16), avg)
        for bi, h in chains:
            c = st[bi, h]
            c["var"] = _dot((c["hc"] * c["hc"]).astype(BF16), avg)
        for bi, h in chains:
            c = st[bi, h]
            lanes = slice(h * hd, (h + 1) * hd)
            o = qkvo_ref[bi, rows, 3 * dm + h * hd:3 * dm + (h + 1) * hd]
            out = c["hc"] * lax.rsqrt(c["var"] + HN_EPS) * ng_ref[0:1, lanes] * _sigmoid(o.astype(F32))
            hm_ref[bi, rows, lanes] = out.astype(BF16)


def _mlstm(qkvo, gate_rows, gate_cols, c0, n0_rep, m0, norm_g, nh, hd, lc):
    b, s, d4 = qkvo.shape
    dm = nh * hd
    rows = max(lc, min(LANE, s))
    ngr = gate_rows.shape[1]
    ngc = gate_cols.shape[2]
    nb = MLSTM_BATCH_ROWS if b % MLSTM_BATCH_ROWS == 0 else 1
    state = pl.BlockSpec((nb, nh, hd, hd), lambda i, j: (i, 0, 0, 0))
    mspec = pl.BlockSpec((nb, 1, nh), lambda i, j: (i, 0, 0))
    return pl.pallas_call(
        functools.partial(_mlstm_kernel, nh=nh, hd=hd, lc=lc),
        out_shape=[jax.ShapeDtypeStruct((b, s, dm), BF16),
                   jax.ShapeDtypeStruct((b, nh, hd, hd), F32),
                   jax.ShapeDtypeStruct((b, nh, hd, hd), F32),
                   jax.ShapeDtypeStruct((b, 1, nh), F32)],
        grid=(b // nb, s // rows),
        in_specs=[pl.BlockSpec((nb, rows, d4), lambda i, j: (i, j, 0)),
                  pl.BlockSpec((nb, ngr, rows), lambda i, j: (i, 0, j)),
                  pl.BlockSpec((nb, rows, ngc), lambda i, j: (i, j, 0)),
                  state, state, mspec,
                  pl.BlockSpec((1, dm), lambda i, j: (0, 0))],
        out_specs=[pl.BlockSpec((nb, rows, dm), lambda i, j: (i, j, 0)), state, state, mspec],
        compiler_params=_params("parallel", "arbitrary"),
        name="mlstm_scan",
    )(qkvo, gate_rows, gate_cols, c0, n0_rep, m0, norm_g.reshape(1, dm))


def _fox_prompt_kernel(qa_ref, ka_ref, vt_ref, o_ref, m_sc, acc_sc, *, nh, hd, t):
    i = pl.program_id(1)
    q0 = pl.multiple_of(i * t, t)
    w = 2 * hd
    hv = hd + FOX_ONES_ROWS
    key = lax.broadcasted_iota(jnp.int32, (t, t), 0)
    qry = lax.broadcasted_iota(jnp.int32, (t, t), 1)
    causal = key <= qry
    neg_inf = jnp.float32(-jnp.inf)

    def scores(k0):
        return [_dot_nt(ka_ref[0, pl.ds(k0, t), h * w:(h + 1) * w], qa_ref[0, :, h * w:(h + 1) * w])
                for h in range(nh)]

    def weighted_values(k0, ps):
        return [_dot(vt_ref[0, h * hv:(h + 1) * hv, pl.ds(k0, t)], ps[h]) for h in range(nh)]

    ss = scores(q0)
    ms, ps = [], []
    for h in range(nh):
        s = jnp.where(causal, ss[h], neg_inf)
        m = jnp.max(s, axis=0, keepdims=True)
        ms.append(m)
        ps.append(jnp.exp2(s - m).astype(BF16))
    m_sc[...] = jnp.concatenate(ms, axis=0)
    pvs = weighted_values(q0, ps)
    for h in range(nh):
        acc_sc[h * hv:(h + 1) * hv, :] = pvs[h]

    def body(j, carry):
        k0 = pl.multiple_of(j * t, t)
        ss = scores(k0)
        m_all = m_sc[...]
        ms, ps, alphas = [], [], []
        for h in range(nh):
            m_old = m_all[h:h + 1, :]
            m_new = jnp.maximum(m_old, jnp.max(ss[h], axis=0, keepdims=True))
            alphas.append(jnp.exp2(m_old - m_new))
            ps.append(jnp.exp2(ss[h] - m_new).astype(BF16))
            ms.append(m_new)
        m_sc[...] = jnp.concatenate(ms, axis=0)
        pvs = weighted_values(k0, ps)
        for h in range(nh):
            rows = slice(h * hv, (h + 1) * hv)
            acc_sc[rows, :] = alphas[h] * acc_sc[rows, :] + pvs[h]
        return carry

    lax.fori_loop(0, i, body, 0)
    for h in range(nh):
        o_ref[0, h * hd:(h + 1) * hd, :] = (acc_sc[h * hv:h * hv + hd, :]
                                            / acc_sc[h * hv + hd:h * hv + hd + 1, :]).astype(BF16)


def _fox_prompt(qa, ka, vt, nh, hd):
    b, s, w2 = qa.shape
    df = nh * hd
    dv = vt.shape[1]
    t = min(FOX_TILE, s)
    return pl.pallas_call(
        functools.partial(_fox_prompt_kernel, nh=nh, hd=hd, t=t),
        out_shape=jax.ShapeDtypeStruct((b, df, s), BF16),
        grid=(b, s // t),
        in_specs=[pl.BlockSpec((1, t, w2), lambda i, j: (i, j, 0)),
                  pl.BlockSpec((1, s, w2), lambda i, j: (i, 0, 0)),
                  pl.BlockSpec((1, dv, s), lambda i, j: (i, 0, 0))],
        out_specs=pl.BlockSpec((1, df, t), lambda i, j: (i, 0, j)),
        scratch_shapes=[pltpu.VMEM((nh, t), F32), pltpu.VMEM((dv, t), F32)],
        compiler_params=_params("parallel", "arbitrary"),
        name="fox_prompt",
    )(qa, ka, vt)


def _fox_sample_kernel(q_ref, kn_ref, vn_ref, kc_ref, vc_ref, fc_ref, gate_ref, o_ref, *, nh, hd, frow):
    t = q_ref.shape[1]
    p_len = kc_ref.shape[3]
    w = min(LANE, p_len)
    r = lax.broadcasted_iota(jnp.int32, (w, w), 0)
    c = lax.broadcasted_iota(jnp.int32, (w, w), 1)
    u_all = (r <= c).astype(BF16)
    fc = fc_ref[0, 0]
    parts = []
    off = jnp.zeros((nh, 1), F32)
    for kk in range(p_len // w):
        cs = _dot_f32_by_01(fc[:, kk * w:(kk + 1) * w], u_all) + off
        off = cs[:, w - 1:w]
        parts.append(cs)
    bias_c_all = off - jnp.concatenate(parts, axis=-1)
    bias_n_all = -gate_ref[0, frow:frow + nh, :]
    row = lax.broadcasted_iota(jnp.int32, (t, t), 0)
    col = lax.broadcasted_iota(jnp.int32, (t, t), 1)
    causal = col <= row
    neg_inf = jnp.float32(-jnp.inf)
    for h in range(nh):
        lanes = slice(h * hd, (h + 1) * hd)
        q = q_ref[0, :, lanes]
        s_c = _dot(q, kc_ref[0, 0, lanes, :].astype(BF16)) + bias_c_all[h:h + 1, :]
        s_n = _dot_nt(q, kn_ref[0, :, lanes]) + bias_n_all[h:h + 1, :]
        s_n = jnp.where(causal, s_n, neg_inf)
        m = jnp.maximum(jnp.max(s_c, axis=-1, keepdims=True), jnp.max(s_n, axis=-1, keepdims=True))
        p_c = jnp.exp(s_c - m)
        p_n = jnp.exp(s_n - m)
        l = jnp.sum(p_c, axis=-1, keepdims=True) + jnp.sum(p_n, axis=-1, keepdims=True)
        acc = (_dot_nt(p_c.astype(BF16), vc_ref[0, 0, lanes, :].astype(BF16))
               + _dot(p_n.astype(BF16), vn_ref[0, :, lanes]))
        o_ref[0, :, lanes] = (acc / l).astype(BF16)


def _fox_sample(q, k_new, v_new, k_cache_t, v_cache_t, logf_cache_t, layer, gate_rows, nh, hd, frow):
    b, t, df = q.shape
    p_len = k_cache_t.shape[3]
    ngr = gate_rows.shape[1]
    blk = lambda n, width: pl.BlockSpec((1, n, width), lambda i: (i, 0, 0))
    cache = lambda n: pl.BlockSpec((1, 1, n, p_len), lambda i: (layer, i, 0, 0))
    return pl.pallas_call(
        functools.partial(_fox_sample_kernel, nh=nh, hd=hd, frow=frow),
        out_shape=jax.ShapeDtypeStruct((b, t, df), BF16),
        grid=(b,),
        in_specs=[blk(t, df), blk(t, df), blk(t, df), cache(df), cache(df), cache(nh), blk(ngr, t)],
        out_specs=blk(t, df),
        compiler_params=_params("parallel"),
        name="fox_sample",
    )(q, k_new, v_new, k_cache_t, v_cache_t, logf_cache_t, gate_rows)


def _tail_kernel(x_ref, mod_ref, hm_ref, hf_ref, hg_ref, wo_ref, l1g_ref, l1b_ref, wg_ref, wu_ref, wd_ref,
                 l2g_ref, l2b_ref, o_ref, *, alpha, fchunk, hf_transposed):
    bb, ts, d = x_ref.shape
    dmm, dgg = hm_ref.shape[2], hg_ref.shape[2]
    dff = hf_ref.shape[1] if hf_transposed else hf_ref.shape[2]
    mod = mod_ref[...]
    wo_f = wo_ref[dmm:dmm + dff, :]
    f = wg_ref.shape[1]
    nsplit = TAIL_SPLIT if (bb == 1 and ts % (TAIL_SPLIT * 128) == 0) else 1
    tr = ts // nsplit
    n = bb * tr
    groups = [slice(g * tr, (g + 1) * tr) for g in range(nsplit)]

    mixes = []
    for rows in groups:
        mix_f = (_dot_tn(hf_ref[0, :, rows], wo_f) if hf_transposed
                 else _dot(hf_ref[:, rows, :].reshape(n, dff), wo_f))
        mixes.append(_dot(hm_ref[:, rows, :].reshape(n, dmm), wo_ref[0:dmm, :]) + mix_f
                     + _dot(hg_ref[:, rows, :].reshape(n, dgg), wo_ref[dmm + dff:dmm + dff + dgg, :]))
    x1s, h2s = [], []
    for rows, mix in zip(groups, mixes):
        x1 = _layer_norm(alpha * x_ref[:, rows, :] + (1.0 + mod[:, 2:3, :]) * mix.reshape(bb, tr, d),
                         l1g_ref[...], l1b_ref[...], LN_EPS)
        x1s.append(x1)
        h2s.append((x1 * (1.0 + mod[:, 4:5, :]) + mod[:, 3:4, :]).reshape(n, d).astype(BF16))
    ys = [jnp.zeros((n, d), F32) for _ in groups]
    for f0 in range(0, f, fchunk):
        gts = [_dot(h2, wg_ref[:, f0:f0 + fchunk]) for h2 in h2s]
        ups = [_dot(h2, wu_ref[:, f0:f0 + fchunk]) for h2 in h2s]
        acts = [(gt * _sigmoid(gt) * up).astype(BF16) for gt, up in zip(gts, ups)]
        ys = [y + _dot(act, wd_ref[f0:f0 + fchunk, :]) for y, act in zip(ys, acts)]
    for rows, x1, y in zip(groups, x1s, ys):
        o_ref[:, rows, :] = _layer_norm(alpha * x1 + (1.0 + mod[:, 5:6, :]) * y.reshape(bb, tr, d),
                                        l2g_ref[...], l2b_ref[...], LN_EPS)


def _ffn_chunk(f):
    for c in (704, 512, 384, 256, 128):
        if f % c == 0:
            return c
    return f


def _layer_tail(x, mod, hm, hf, hg, wo, l1g, l1b, wg, wu, wd, l2g, l2b, bb, ts, alpha, hf_transposed):
    b, s, d = x.shape
    f = wg.shape[1]
    row = lambda width: pl.BlockSpec((bb, ts, width), lambda i, j: (i, j, 0))
    vec = lambda a: a.reshape(1, d)
    if hf_transposed:
        assert bb == 1
        hf_spec = pl.BlockSpec((1, hf.shape[1], ts), lambda i, j: (i, 0, j))
    else:
        hf_spec = row(hf.shape[2])
    return pl.pallas_call(
        functools.partial(_tail_kernel, alpha=alpha, fchunk=_ffn_chunk(f), hf_transposed=hf_transposed),
        out_shape=jax.ShapeDtypeStruct((b, s, d), F32),
        grid=(b // bb, s // ts),
        in_specs=[row(d), pl.BlockSpec((bb, 6, d), lambda i, j: (i, 0, 0)),
                  row(hm.shape[2]), hf_spec, row(hg.shape[2]),
                  _resident(wo.shape), _resident((1, d)), _resident((1, d)),
                  _resident(wg.shape), _resident(wu.shape), _resident(wd.shape),
                  _resident((1, d)), _resident((1, d))],
        out_specs=row(d),
        compiler_params=_params("parallel", "parallel"),
        name="layer_tail",
    )(x, mod, hm, hf, hg, wo, vec(l1g), vec(l1b), wg, wu, wd, vec(l2g), vec(l2b))


def _tile(b, s):
    if s >= ROW_TILE:
        return 1, ROW_TILE
    bb = max(1, min(b, ROW_TILE // s))
    while b % bb:
        bb -= 1
    return bb, s


def _layer(x, mod, p, cache, dims, kv_prev=None):
    nh_m, nh_f, ng, hd = dims
    b, s, d = x.shape
    dm, df, dg = nh_m * hd, nh_f * hd, ng * hd
    bb, ts = _tile(b, s)
    alpha = p["alpha"]
    lc = min(s, p["gmlp_ws"].shape[-1])
    gm = (p["gmlp_ws"], p["gmlp_bs_t"], p["gmlp_ln_g"], p["gmlp_ln_b"])
    nrow = 2 * nh_m + nh_f

    if cache is None:
        gate_bias_row = jnp.pad(p["gate_bias"].reshape(1, nrow), ((0, 0), (0, LANE - nrow)))
        m_all, qa, ka, vte, k_t, v_t, hg, gates = _in_projection_prompt(
            x, mod, p["w_in"], gate_bias_row, gm, kv_prev, p["layer"], p["depth"], ts, 4 * dm, df, 2 * dg,
            hd ** -0.5 * LOG2E, ng, hd, lc, nh_f, 2 * nh_m)
    else:
        m_all, fq, fk, fv, fkb, fvb, hg, vn, gates = _in_projection_sample(
            x, mod, p["w_in"], gm, bb, ts, 4 * dm, df, 2 * dg, hd ** -0.5, ng, hd, lc)

    gates_t = jnp.transpose(gates[:, :, :nrow], (0, 2, 1))
    lm = s if cache is not None else min(MLSTM_PROMPT_CHUNK, s)
    gate_rows = _gate_prepass(gates_t, p["gate_bias"], nh_m, nh_f, lm)
    logf = jnp.transpose(gate_rows[:, 2 * nh_m:2 * nh_m + nh_f, :], (0, 2, 1))
    frow = 2 * nh_m + nh_f

    if cache is None:
        c0 = jnp.zeros((b, nh_m, hd, hd), F32)
        n0 = jnp.zeros((b, nh_m, hd, hd), F32)
        m0 = jnp.full((b, 1, nh_m), -jnp.inf, F32)
    else:
        k_c, v_c, logf_c, c_st, n_st, m_st = cache
        c0 = c_st
        n0 = jnp.broadcast_to(n_st[..., None], (b, nh_m, hd, hd))
        m0 = m_st.reshape(b, 1, nh_m)
    gate_cols = jnp.transpose(gate_rows[:, :2 * nh_m, :], (0, 2, 1))
    hm, c1, n1, m1 = _mlstm(m_all, gate_rows, gate_cols, c0, n0, m0, p["mlstm_norm_g"], nh_m, hd, lm)

    if cache is None:
        hf = _fox_prompt(qa, ka, vte, nh_f, hd)
    else:
        hf = _fox_sample(fq, fkb, fvb, k_c, v_c, logf_c, p["layer"], gate_rows, nh_f, hd, frow)

    x_out = _layer_tail(x, mod, hm, hf, hg, p["w_o"], p["ln1_g"], p["ln1_b"], p["w_gate"], p["w_up"],
                        p["w_down"], p["ln2_g"], p["ln2_b"], bb, ts, alpha, cache is None)
    if cache is None:
        new = (k_t, v_t, logf, c1, n1[..., 0], m1.reshape(b, nh_m))
    else:
        new = (fk.reshape(b, s, nh_f, hd), fv.reshape(b, s, nh_f, hd), logf,
               c1, n1[..., 0], m1.reshape(b, nh_m), vn)
    return x_out, new


def kernel(x_prompt, x_sample, cache_fox_k, cache_fox_v, cache_fox_logf, state_mlstm_C, state_mlstm_n, state_mlstm_m, c_prompt, c_sample, w_ada, b_ada, w_in, b_mlstm_i, b_mlstm_f, mlstm_norm_g, b_fox_f, gmlp_ln_g, gmlp_ln_b, gmlp_ws, gmlp_bs, w_o, ln1_g, ln1_b, w_gate, w_up, w_down, ln2_g, ln2_b):
    depth = w_in.shape[0]
    nh_m = b_mlstm_i.shape[1]
    nh_f = b_fox_f.shape[1]
    ng = gmlp_ws.shape[1]
    hd = cache_fox_k.shape[-1]
    dm, df, dg = nh_m * hd, nh_f * hd, ng * hd
    d = x_prompt.shape[-1]
    bp = x_prompt.shape[0]
    alpha = (2 * depth) ** 0.25

    o_mi = 4 * dm
    o_fq = o_mi + 2 * nh_m
    o_ff = o_fq + 3 * df
    o_gu = o_ff + nh_f
    npad = LANE - (2 * nh_m + nh_f)

    bs, p_len = cache_fox_k.shape[1], cache_fox_k.shape[2]
    cache_kt = jnp.transpose(cache_fox_k, (0, 1, 3, 4, 2)).reshape(depth, bs, df, p_len)
    cache_vt = jnp.transpose(cache_fox_v, (0, 1, 3, 4, 2)).reshape(depth, bs, df, p_len)
    cache_ft = jnp.transpose(cache_fox_logf, (0, 1, 3, 2))

    xp, xs = x_prompt, x_sample
    c_all = jnp.concatenate([c_prompt, c_sample], axis=0)
    new_p, new_s = [], []
    kv_prev = None
    for l in range(depth):
        wl = w_in[l]
        w_all = jnp.concatenate(
            [wl[:, :o_mi], wl[:, o_fq:o_ff], wl[:, o_gu:o_gu + 2 * dg], wl[:, o_mi:o_fq], wl[:, o_ff:o_gu],
             jnp.zeros((d, npad), wl.dtype)], axis=1).astype(BF16)
        p = dict(
            alpha=alpha, w_in=w_all, layer=l, depth=depth,
            gate_bias=jnp.concatenate([b_mlstm_i[l], b_mlstm_f[l], b_fox_f[l]]).reshape(-1, 1),
            mlstm_norm_g=mlstm_norm_g[l], gmlp_ln_g=gmlp_ln_g[l], gmlp_ln_b=gmlp_ln_b[l],
            gmlp_ws=gmlp_ws[l], gmlp_bs_t=jnp.transpose(gmlp_bs[l]),
            w_o=w_o[l].astype(BF16), ln1_g=ln1_g[l], ln1_b=ln1_b[l],
            w_gate=w_gate[l].astype(BF16), w_up=w_up[l].astype(BF16), w_down=w_down[l].astype(BF16),
            ln2_g=ln2_g[l], ln2_b=ln2_b[l])
        mod = _modulation(c_all, w_ada, b_ada, l).reshape(c_all.shape[0], 6, d)
        xp, st_p = _layer(xp, mod[:bp], p, None, (nh_m, nh_f, ng, hd), kv_prev)
        kv_prev = st_p[:2]
        xs, st_s = _layer(xs, mod[bp:], p, (cache_kt, cache_vt, cache_ft,
                                            state_mlstm_C[l], state_mlstm_n[l], state_mlstm_m[l]),
                          (nh_m, nh_f, ng, hd))
        new_p.append(st_p)
        new_s.append(st_s)

    def stk(lst, i):
        return jnp.stack([e[i] for e in lst], axis=0)

    def heads_last(kv_t):
        sp = kv_t.shape[-1]
        return jnp.transpose(kv_t.reshape(depth, bp, nh_f, hd, sp), (0, 1, 4, 2, 3))

    return (xp, xs,
            heads_last(kv_prev[0]), heads_last(kv_prev[1]), stk(new_p, 2), stk(new_p, 3), stk(new_p, 4), stk(new_p, 5),
            stk(new_s, 0), stk(new_s, 1), stk(new_s, 2), stk(new_s, 3), stk(new_s, 4), stk(new_s, 5),
            stk(new_s, 6))
```

```python
import functools

import jax
import jax.numpy as jnp
from jax import lax
from jax.experimental import pallas as pl
from jax.experimental.pallas import tpu as pltpu

F32 = jnp.float32
BF16 = jnp.bfloat16

LOG2E = 1.4426950408889634
LN_EPS = 1e-5
HN_EPS = 1e-6
MLSTM_PROMPT_CHUNK = 256
MLSTM_BATCH_ROWS = 4
ROW_TILE = 512
TAIL_SPLIT = 2
FOX_TILE = 512
FOX_SKEW = 1
FOX_ONES_ROWS = 16
LANE = 128
VMEM_LIMIT = 56 * 1024 * 1024


def _sigmoid(x):
    return 1.0 / (1.0 + jnp.exp(-x))


def _log_sigmoid(x):
    return jnp.minimum(x, 0.0) - jnp.log1p(jnp.exp(-jnp.abs(x)))


def _dot(a, b):
    return jnp.dot(a, b, preferred_element_type=F32)


def _dot_nt(a, b):
    return lax.dot_general(a, b, (((1,), (1,)), ((), ())), preferred_element_type=F32)


def _dot_tn(a, b):
    return lax.dot_general(a, b, (((0,), (0,)), ((), ())), preferred_element_type=F32)


def _dot_f32_by_01(x, u):
    hi = x.astype(BF16)
    r1 = x - hi.astype(F32)
    mid = r1.astype(BF16)
    lo = (r1 - mid.astype(F32)).astype(BF16)
    return _dot(hi, u) + _dot(mid, u) + _dot(lo, u)


def _layer_norm(x, g, b, eps):
    mu = jnp.mean(x, axis=-1, keepdims=True)
    xc = x - mu
    var = jnp.mean(xc * xc, axis=-1, keepdims=True)
    return xc * lax.rsqrt(var + eps) * g + b


def _params(*sem):
    return pltpu.CompilerParams(dimension_semantics=sem, vmem_limit_bytes=VMEM_LIMIT)


def _resident(shape):
    nd = len(shape)
    return pl.BlockSpec(shape, lambda *_: (0,) * nd, pipeline_mode=pl.Buffered(1))


def _mod_kernel(c_ref, w_ref, b_ref, o_ref):
    c = c_ref[...]
    s = (c * _sigmoid(c)).astype(BF16)
    o_ref[...] = _dot(s, w_ref[0].astype(BF16)) + b_ref[0]


def _modulation(c, w_ada, b_ada, layer):
    nb, d = c.shape
    depth, _, n = w_ada.shape
    tn = n // 4 if (n // 4) % LANE == 0 else n
    return pl.pallas_call(
        _mod_kernel,
        out_shape=jax.ShapeDtypeStruct((nb, n), F32),
        grid=(n // tn,),
        in_specs=[pl.BlockSpec((nb, d), lambda j: (0, 0)),
                  pl.BlockSpec((1, d, tn), lambda j: (layer, 0, j)),
                  pl.BlockSpec((1, 1, tn), lambda j: (layer, 0, j))],
        out_specs=pl.BlockSpec((nb, tn), lambda j: (0, j)),
        compiler_params=_params("parallel"),
        name="adaln_mod",
    )(c, w_ada, b_ada.reshape(depth, 1, n))


def _modulated(x_ref, mod_ref):
    bb, ts, d = x_ref.shape
    mod = mod_ref[...]
    h = x_ref[...] * (1.0 + mod[:, 1:2, :]) + mod[:, 0:1, :]
    return h.reshape(bb * ts, d).astype(BF16)


def _spatial_gate(gall, gw_ref, gbs_ref, lng_ref, lnb_ref, hg_ref, *, ng, hd, lc):
    bb, ts, dg = hg_ref.shape
    n = bb * ts
    gu = gall[:, :dg]
    vn = _layer_norm(gall[:, dg:], lng_ref[...], lnb_ref[...], LN_EPS)
    row = lax.broadcasted_iota(jnp.int32, (lc, lc), 0)
    col = lax.broadcasted_iota(jnp.int32, (lc, lc), 1)
    tri = col <= row
    per_row = ts // lc
    zs = {}
    for g in range(ng):
        lanes = slice(g * hd, (g + 1) * hd)
        wg = jnp.where(tri, gw_ref[g, 0:lc, 0:lc], 0.0).astype(BF16)
        for ci in range(n // lc):
            zs[g, ci] = _dot(wg, vn[ci * lc:(ci + 1) * lc, lanes].astype(BF16))
    for g in range(ng):
        lanes = slice(g * hd, (g + 1) * hd)
        bcol = gbs_ref[0:lc, g:g + 1]
        for ci in range(n // lc):
            rows = slice(ci * lc, (ci + 1) * lc)
            bi, ri = ci // per_row, ci % per_row
            hg_ref[bi, ri * lc:(ri + 1) * lc, lanes] = (gu[rows, lanes] * (zs[g, ci] + bcol)).astype(BF16)
    return vn


def _inproj_sample_kernel(x_ref, mod_ref, w_ref, gw_ref, gbs_ref, lng_ref, lnb_ref,
                          m_ref, fq_ref, fk_ref, fv_ref, fkb_ref, fvb_ref, hg_ref, vn_ref, gate_ref,
                          *, dm4, df, dg2, q_scale, ng, hd, lc):
    bb, ts, _ = x_ref.shape
    h = _modulated(x_ref, mod_ref)

    def seg(c0, width):
        return _dot(h, w_ref[:, c0:c0 + width])

    m_ref[...] = seg(0, dm4).astype(BF16).reshape(bb, ts, dm4)
    c0 = dm4
    fq_ref[...] = (seg(c0, df) * q_scale).astype(BF16).reshape(bb, ts, df)
    fk = seg(c0 + df, df)
    fk_ref[...] = fk.reshape(bb, ts, df)
    fkb_ref[...] = fk.astype(BF16).reshape(bb, ts, df)
    fv = seg(c0 + 2 * df, df)
    fv_ref[...] = fv.reshape(bb, ts, df)
    fvb_ref[...] = fv.astype(BF16).reshape(bb, ts, df)
    c0 += 3 * df
    gate_ref[...] = seg(c0 + dg2, LANE).reshape(bb, ts, LANE)
    vn = _spatial_gate(seg(c0, dg2), gw_ref, gbs_ref, lng_ref, lnb_ref, hg_ref, ng=ng, hd=hd, lc=lc)
    vn_ref[...] = vn.reshape(bb, ts, dg2 // 2)


def _inproj_prompt_kernel(x_ref, mod_ref, w_ref, gbias_ref, tril_ref, gw_ref, gbs_ref, lng_ref, lnb_ref, *rest,
                          dm4, df, dg2, q_scale, ng, hd, lc, nh, fcol, aliased):
    if aliased:
        rest = rest[2:]
    m_ref, qa_ref, ka_ref, vte_ref, kt_ref, vt_ref, hg_ref, gate_ref, carry = rest
    _, ts, _ = x_ref.shape

    @pl.when(pl.program_id(1) == 0)
    def _():
        carry[...] = jnp.zeros_like(carry)

    h = _modulated(x_ref, mod_ref)

    def seg(c0, width):
        return _dot(h, w_ref[:, c0:c0 + width])

    c_f = dm4
    c_g = dm4 + 3 * df
    gt = seg(c_g + dg2, LANE)
    gate_ref[0] = gt

    logf = _log_sigmoid(gt + gbias_ref[...])
    lower = tril_ref[...]
    hi = logf.astype(BF16)
    r1 = logf - hi.astype(F32)
    mid = r1.astype(BF16)
    lo = (r1 - mid.astype(F32)).astype(BF16)
    fcum = _dot(lower, hi) + _dot(lower, mid) + _dot(lower, lo) + carry[...]
    carry[...] = fcum[ts - 1:ts, :]
    nf = fcum * (-LOG2E)

    fk = seg(c_f + df, df)
    fv = seg(c_f + 2 * df, df)
    kt_ref[0, 0] = jnp.transpose(fk)
    vt = jnp.transpose(fv)
    vt_ref[0, 0] = vt
    if not aliased and kt_ref.shape[0] > 1:
        kt_ref[1:, 0] = jnp.zeros((kt_ref.shape[0] - 1,) + kt_ref.shape[2:], F32)
        vt_ref[1:, 0] = jnp.zeros((vt_ref.shape[0] - 1,) + vt_ref.shape[2:], F32)
    ones = jnp.ones((FOX_ONES_ROWS, ts), F32)
    v_parts = []
    for hh in range(nh):
        v_parts += [vt[hh * hd:(hh + 1) * hd, :], ones]
    vte_ref[0] = jnp.concatenate(v_parts, axis=0).astype(BF16)

    fq = (seg(c_f, df) * q_scale).astype(BF16)
    lane = lax.broadcasted_iota(jnp.int32, (ts, hd), 1)
    ones3 = (lane < 3).astype(BF16)
    fkb = fk.astype(BF16)
    q_parts, k_parts = [], []
    for hh in range(nh):
        lanes = slice(hh * hd, (hh + 1) * hd)
        x = nf[:, fcol + hh:fcol + hh + 1]
        xh = x.astype(BF16).astype(F32)
        x1 = x - xh
        xm = x1.astype(BF16).astype(F32)
        extra = jnp.where(lane == 0, xh, jnp.where(lane == 1, xm, jnp.where(lane == 2, x1 - xm, 0.0)))
        q_parts += [fq[:, lanes], ones3]
        k_parts += [fkb[:, lanes], extra.astype(BF16)]
    gall = seg(c_g, dg2)
    ka_ref[0] = jnp.concatenate(k_parts, axis=-1)
    qa_ref[0] = jnp.concatenate(q_parts, axis=-1)
    m_lo = seg(0, dm4 // 2).astype(BF16)
    _spatial_gate(gall, gw_ref, gbs_ref, lng_ref, lnb_ref, hg_ref, ng=ng, hd=hd, lc=lc)
    m_hi = seg(dm4 // 2, dm4 // 2).astype(BF16)
    m_ref[0, :, 0:dm4 // 2] = m_lo
    m_ref[0, :, dm4 // 2:dm4] = m_hi


def _gmlp_specs(gw, gbs_t, dg):
    return [_resident(gw.shape), _resident(gbs_t.shape), _resident((1, dg)), _resident((1, dg))]


def _in_projection_sample(x, mod, w_all, gm, bb, ts, dm4, df, dg2, q_scale, ng, hd, lc):
    b, s, d = x.shape
    dg = dg2 // 2
    gw, gbs_t, lng, lnb = gm
    row = lambda width: pl.BlockSpec((bb, ts, width), lambda i, j: (i, j, 0))
    shp = lambda width, dt: jax.ShapeDtypeStruct((b, s, width), dt)
    return pl.pallas_call(
        functools.partial(_inproj_sample_kernel, dm4=dm4, df=df, dg2=dg2, q_scale=q_scale, ng=ng, hd=hd, lc=lc),
        out_shape=[shp(dm4, BF16), shp(df, BF16), shp(df, F32), shp(df, F32), shp(df, BF16), shp(df, BF16),
                   shp(dg, BF16), shp(dg, F32), shp(LANE, F32)],
        grid=(b // bb, s // ts),
        in_specs=[row(d), pl.BlockSpec((bb, 6, d), lambda i, j: (i, 0, 0)), _resident(w_all.shape)]
        + _gmlp_specs(gw, gbs_t, dg),
        out_specs=[row(dm4), row(df), row(df), row(df), row(df), row(df), row(dg), row(dg), row(LANE)],
        compiler_params=_params("parallel", "parallel"),
        name="in_proj_sample",
    )(x, mod, w_all, gw, gbs_t, lng.reshape(1, dg), lnb.reshape(1, dg))


def _in_projection_prompt(x, mod, w_all, gate_bias_row, gm, kv_prev, layer, depth, ts, dm4, df, dg2, q_scale,
                          ng, hd, lc, nh, fcol):
    b, s, d = x.shape
    dg = dg2 // 2
    dv = nh * (hd + FOX_ONES_ROWS)
    gw, gbs_t, lng, lnb = gm
    aliased = kv_prev is not None
    row = lambda width: pl.BlockSpec((1, ts, width), lambda i, j: (i, j, 0))
    shp = lambda width, dt: jax.ShapeDtypeStruct((b, s, width), dt)
    kv_shape = jax.ShapeDtypeStruct((depth, b, df, s), F32)
    if aliased:
        kv_spec = pl.BlockSpec((1, 1, df, ts), lambda i, j: (layer, i, 0, j))
    else:
        assert layer == 0
        kv_spec = pl.BlockSpec((depth, 1, df, ts), lambda i, j: (0, i, 0, j))
    in_specs = ([row(d), pl.BlockSpec((1, 6, d), lambda i, j: (i, 0, 0)), _resident(w_all.shape),
                 _resident((1, LANE)), _resident((ts, ts))] + _gmlp_specs(gw, gbs_t, dg))
    tril = jnp.tril(jnp.ones((ts, ts), BF16))
    args = [x, mod, w_all, gate_bias_row, tril, gw, gbs_t, lng.reshape(1, dg), lnb.reshape(1, dg)]
    aliases = {}
    if aliased:
        in_specs += [pl.BlockSpec(memory_space=pl.ANY), pl.BlockSpec(memory_space=pl.ANY)]
        aliases = {len(args): 4, len(args) + 1: 5}
        args += list(kv_prev)
    return pl.pallas_call(
        functools.partial(_inproj_prompt_kernel, dm4=dm4, df=df, dg2=dg2, q_scale=q_scale, ng=ng, hd=hd, lc=lc,
                          nh=nh, fcol=fcol, aliased=aliased),
        out_shape=[shp(dm4, BF16), shp(2 * df, BF16), shp(2 * df, BF16),
                   jax.ShapeDtypeStruct((b, dv, s), BF16), kv_shape, kv_shape, shp(dg, BF16), shp(LANE, F32)],
        grid=(b, s // ts),
        in_specs=in_specs,
        out_specs=[row(dm4), row(2 * df), row(2 * df), pl.BlockSpec((1, dv, ts), lambda i, j: (i, 0, j)),
                   kv_spec, kv_spec, row(dg), row(LANE)],
        scratch_shapes=[pltpu.VMEM((1, LANE), F32)],
        input_output_aliases=aliases,
        compiler_params=_params("parallel", "arbitrary"),
        name="in_proj_prompt",
    )(*args)


def _gates_kernel(g_ref, bias_ref, o_ref, *, hm, hf, lm):
    s = g_ref.shape[2]
    w = min(LANE, s)
    x = g_ref[0] + bias_ref[...]
    ls = _log_sigmoid(x)
    r = lax.broadcasted_iota(jnp.int32, (w, w), 0)
    c = lax.broadcasted_iota(jnp.int32, (w, w), 1)
    upper = r <= c
    u_all = upper.astype(BF16)
    u_chunk = jnp.logical_and(upper, (r // lm) == (c // lm)).astype(BF16)
    off = jnp.zeros((hf, 1), F32)
    b_off = jnp.zeros((hm, 1), F32)
    for k in range(s // w):
        cols = slice(k * w, (k + 1) * w)
        li = x[0:hm, cols]
        if (k * w) % lm == 0:
            b_off = jnp.zeros((hm, 1), F32)
        b = _dot_f32_by_01(ls[hm:2 * hm, cols], u_chunk) + b_off
        if lm > w:
            b_off = b[:, w - 1:w]
        logf = ls[2 * hm:2 * hm + hf, cols]
        fcum = _dot_f32_by_01(logf, u_all) + off
        off = fcum[:, w - 1:w]
        o_ref[0, 0:hm, cols] = li - b
        o_ref[0, hm:2 * hm, cols] = b
        o_ref[0, 2 * hm:2 * hm + hf, cols] = logf
        o_ref[0, 2 * hm + hf:2 * hm + 2 * hf, cols] = fcum


def _gate_prepass(gates_t, bias_col, hm, hf, lm):
    b, nr, s = gates_t.shape
    nout = 2 * hm + 2 * hf
    return pl.pallas_call(
        functools.partial(_gates_kernel, hm=hm, hf=hf, lm=lm),
        out_shape=jax.ShapeDtypeStruct((b, nout, s), F32),
        grid=(b,),
        in_specs=[pl.BlockSpec((1, nr, s), lambda i: (i, 0, 0)), pl.BlockSpec((nr, 1), lambda i: (0, 0))],
        out_specs=pl.BlockSpec((1, nout, s), lambda i: (i, 0, 0)),
        compiler_params=_params("parallel"),
        name="gate_prepass",
    )(gates_t, bias_col)


def _dot_f32_2(x, w):
    hi = x.astype(BF16)
    lo = (x - hi.astype(F32)).astype(BF16)
    return _dot(hi, w) + _dot(lo, w)


def _mlstm_kernel(qkvo_ref, gate_ref, gcol_ref, c0_ref, n0_ref, m0_ref, ng_ref, hm_ref, c_ref, n_ref, m_ref,
                  *, nh, hd, lc):
    @pl.when(pl.program_id(1) == 0)
    def _():
        c_ref[...] = c0_ref[...]
        n_ref[...] = n0_ref[...]
        m_ref[...] = m0_ref[...]

    nb = qkvo_ref.shape[0]
    rows_per_step = qkvo_ref.shape[1]
    dm = nh * hd
    ksc = hd ** -0.5
    row = lax.broadcasted_iota(jnp.int32, (lc, lc), 0)
    col = lax.broadcasted_iota(jnp.int32, (lc, lc), 1)
    tri = col <= row
    neg_inf = jnp.float32(-jnp.inf)
    ones_v = jnp.ones((lc, hd), BF16)
    avg = jnp.full((hd, hd), 1.0 / hd, BF16)

    chains = [(bi, h) for bi in range(nb) for h in range(nh)]
    for ci in range(rows_per_step // lc):
        rows = slice(ci * lc, (ci + 1) * lc)
        m_olds = [m_ref[bi] for bi in range(nb)]
        st = {}
        for bi, h in chains:
            q = qkvo_ref[bi, rows, h * hd:(h + 1) * hd]
            k = qkvo_ref[bi, rows, dm + h * hd:dm + (h + 1) * hd]
            a_row = gate_ref[bi, h:h + 1, rows]
            m0 = m_olds[bi][:, h:h + 1]
            c_old = c_ref[bi, h]
            n_old = n_ref[bi, h]
            a_full = jnp.broadcast_to(a_row, (lc, lc))
            g_col = jnp.maximum(jnp.max(jnp.where(tri, a_full, neg_inf), axis=-1, keepdims=True), m0)
            st[bi, h] = dict(k=k, m0=m0, c_old=c_old, n_old=n_old, a_full=a_full, g_col=g_col,
                             qk=_dot_nt(q, k), qc=_dot(q, c_old.astype(BF16)), qn=_dot(q, n_old.astype(BF16)))
        for bi, h in chains:
            c = st[bi, h]
            v = qkvo_ref[bi, rows, 2 * dm + h * hd:2 * dm + (h + 1) * hd]
            g_b = jnp.broadcast_to(c["g_col"], (lc, max(lc, hd)))
            a_b = jnp.broadcast_to(gcol_ref[bi, rows, h:h + 1], (lc, hd))
            b_b = jnp.broadcast_to(gcol_ref[bi, rows, nh + h:nh + h + 1], (lc, hd))
            g_h = g_b[:, :hd]
            w_intra = jnp.exp(jnp.where(tri, c["a_full"] - g_b[:, :lc], neg_inf))
            c["sc"] = (c["qk"] * (w_intra * ksc)).astype(BF16)
            c["w_inter"] = jnp.exp(c["m0"] - g_h)
            c["floor"] = jnp.exp(-(b_b + g_h))
            c["g_end"] = c["g_col"][lc - 1:lc, :]
            w_end = jnp.exp(a_b - c["g_end"]) * ksc
            c["w_end"] = w_end.astype(BF16)
            c["vw"] = v * c["w_end"]
            c["v"] = v
        m_new = [[] for _ in range(nb)]
        for bi, h in chains:
            c = st[bi, h]
            decay = jnp.exp(c["m0"] - c["g_end"])
            c_ref[bi, h] = decay * c["c_old"] + _dot_tn(c["k"], c["vw"])
            n_ref[bi, h] = decay * c["n_old"] + _dot_tn(c["k"], c["w_end"])
            b_row = gate_ref[bi, nh + h:nh + h + 1, rows]
            m_new[bi].append(b_row[:, lc - 1:lc] + c["g_end"])
        for bi in range(nb):
            m_ref[bi] = jnp.concatenate(m_new[bi], axis=-1)
        for bi, h in chains:
            c = st[bi, h]
            num = _dot(c["sc"], c["v"]) + c["qc"] * c["w_inter"]
            den = _dot(c["sc"], ones_v) + c["qn"] * c["w_inter"]
            c["hh"] = num / jnp.maximum(jnp.abs(den), c["floor"])
        for bi, h in chains:
            c = st[bi, h]
            c["hc"] = c["hh"] - _dot(c["hh"].astype(BF16), avg)
        for bi, h in chains:
            c = st[bi, h]
            c["var"] = _dot((c["hc"] * c["hc"]).astype(BF16), avg)
        for bi, h in chains:
            c = st[bi, h]
            lanes = slice(h * hd, (h + 1) * hd)
            o = qkvo_ref[bi, rows, 3 * dm + h * hd:3 * dm + (h + 1) * hd]
            out = c["hc"] * lax.rsqrt(c["var"] + HN_EPS) * ng_ref[0:1, lanes] * _sigmoid(o.astype(F32))
            hm_ref[bi, rows, lanes] = out.astype(BF16)


def _mlstm(qkvo, gate_rows, gate_cols, c0, n0_rep, m0, norm_g, nh, hd, lc):
    b, s, d4 = qkvo.shape
    dm = nh * hd
    rows = max(lc, min(LANE, s))
    ngr = gate_rows.shape[1]
    ngc = gate_cols.shape[2]
    nb = MLSTM_BATCH_ROWS if b % MLSTM_BATCH_ROWS == 0 else 1
    state = pl.BlockSpec((nb, nh, hd, hd), lambda i, j: (i, 0, 0, 0))
    mspec = pl.BlockSpec((nb, 1, nh), lambda i, j: (i, 0, 0))
    return pl.pallas_call(
        functools.partial(_mlstm_kernel, nh=nh, hd=hd, lc=lc),
        out_shape=[jax.ShapeDtypeStruct((b, s, dm), BF16),
                   jax.ShapeDtypeStruct((b, nh, hd, hd), F32),
                   jax.ShapeDtypeStruct((b, nh, hd, hd), F32),
                   jax.ShapeDtypeStruct((b, 1, nh), F32)],
        grid=(b // nb, s // rows),
        in_specs=[pl.BlockSpec((nb, rows, d4), lambda i, j: (i, j, 0)),
                  pl.BlockSpec((nb, ngr, rows), lambda i, j: (i, 0, j)),
                  pl.BlockSpec((nb, rows, ngc), lambda i, j: (i, j, 0)),
                  state, state, mspec,
                  pl.BlockSpec((1, dm), lambda i, j: (0, 0))],
        out_specs=[pl.BlockSpec((nb, rows, dm), lambda i, j: (i, j, 0)), state, state, mspec],
        compiler_params=_params("parallel", "arbitrary"),
        name="mlstm_scan",
    )(qkvo, gate_rows, gate_cols, c0, n0_rep, m0, norm_g.reshape(1, dm))


def _fox_prompt_kernel(qa_ref, ka_ref, vt_ref, o_ref, m_sc, acc_sc, *, nh, hd, t):
    i = pl.program_id(1)
    q0 = pl.multiple_of(i * t, t)
    w = 2 * hd
    hv = hd + FOX_ONES_ROWS
    neg_inf = jnp.float32(-jnp.inf)

    th = t // 2
    q1 = pl.multiple_of(q0 + th, th)
    vis_a = lax.broadcasted_iota(jnp.int32, (th, t), 0) <= lax.broadcasted_iota(jnp.int32, (th, t), 1)
    vis_b = lax.broadcasted_iota(jnp.int32, (th, th), 0) <= lax.broadcasted_iota(jnp.int32, (th, th), 1)
    s_a = [_dot_nt(ka_ref[0, pl.ds(q0, th), h * w:(h + 1) * w], qa_ref[0, :, h * w:(h + 1) * w])
           for h in range(nh)]
    s_b = [_dot_nt(ka_ref[0, pl.ds(q1, th), h * w:(h + 1) * w], qa_ref[0, th:, h * w:(h + 1) * w])
           for h in range(nh)]
    ms, p_a, p_b = [], [], []
    for h in range(nh):
        sa = jnp.where(vis_a, s_a[h], neg_inf)
        sb = jnp.where(vis_b, s_b[h], neg_inf)
        m_a = jnp.max(sa, axis=0, keepdims=True)
        m_hi = jnp.maximum(m_a[:, th:], jnp.max(sb, axis=0, keepdims=True))
        m = jnp.concatenate([m_a[:, :th], m_hi], axis=1)
        ms.append(m)
        p_a.append(jnp.exp2(sa - m).astype(BF16))
        p_b.append(jnp.exp2(sb - m_hi).astype(BF16))
    m_sc[...] = jnp.concatenate(ms, axis=0)
    pv_a = [_dot(vt_ref[0, h * hv:(h + 1) * hv, pl.ds(q0, th)], p_a[h]) for h in range(nh)]
    pv_b = [_dot(vt_ref[0, h * hv:(h + 1) * hv, pl.ds(q1, th)], p_b[h]) for h in range(nh)]
    for h in range(nh):
        acc_sc[h * hv:(h + 1) * hv, :th] = pv_a[h][:, :th]
        acc_sc[h * hv:(h + 1) * hv, th:] = pv_a[h][:, th:] + pv_b[h]

    def body(j, carry):
        k0 = pl.multiple_of(j * t, t)
        m_all = m_sc[...]
        ss, ms, ps, alphas = {}, {}, {}, {}
        for step in range(nh + FOX_SKEW + 1):
            h_a, h_b, h_c = step, step - FOX_SKEW, step - FOX_SKEW - 1
            if h_a < nh:
                ss[h_a] = _dot_nt(ka_ref[0, pl.ds(k0, t), h_a * w:(h_a + 1) * w], qa_ref[0, :, h_a * w:(h_a + 1) * w])
            if 0 <= h_b < nh:
                m_old = m_all[h_b:h_b + 1, :]
                m_new = jnp.maximum(m_old, jnp.max(ss[h_b], axis=0, keepdims=True))
                alphas[h_b] = jnp.exp2(m_old - m_new)
                ps[h_b] = jnp.exp2(ss.pop(h_b) - m_new).astype(BF16)
                ms[h_b] = m_new
            if 0 <= h_c < nh:
                rows = slice(h_c * hv, (h_c + 1) * hv)
                pv = _dot(vt_ref[0, rows, pl.ds(k0, t)], ps.pop(h_c))
                acc_sc[rows, :] = alphas[h_c] * acc_sc[rows, :] + pv
        m_sc[...] = jnp.concatenate([ms[h] for h in range(nh)], axis=0)
        return carry

    lax.fori_loop(0, i, body, 0)
    for h in range(nh):
        o_ref[0, h * hd:(h + 1) * hd, :] = (acc_sc[h * hv:h * hv + hd, :]
                                            / acc_sc[h * hv + hd:h * hv + hd + 1, :]).astype(BF16)


def _fox_prompt(qa, ka, vt, nh, hd):
    b, s, w2 = qa.shape
    df = nh * hd
    dv = vt.shape[1]
    t = min(FOX_TILE, s)
    return pl.pallas_call(
        functools.partial(_fox_prompt_kernel, nh=nh, hd=hd, t=t),
        out_shape=jax.ShapeDtypeStruct((b, df, s), BF16),
        grid=(b, s // t),
        in_specs=[pl.BlockSpec((1, t, w2), lambda i, j: (i, j, 0)),
                  pl.BlockSpec((1, s, w2), lambda i, j: (i, 0, 0)),
                  pl.BlockSpec((1, dv, s), lambda i, j: (i, 0, 0))],
        out_specs=pl.BlockSpec((1, df, t), lambda i, j: (i, 0, j)),
        scratch_shapes=[pltpu.VMEM((nh, t), F32), pltpu.VMEM((dv, t), F32)],
        compiler_params=_params("parallel", "arbitrary"),
        name="fox_prompt",
    )(qa, ka, vt)


def _fox_sample_kernel(q_ref, kn_ref, vn_ref, kc_ref, vc_ref, fc_ref, gate_ref, o_ref, *, nh, hd, frow):
    t = q_ref.shape[1]
    p_len = kc_ref.shape[3]
    w = min(LANE, p_len)
    r = lax.broadcasted_iota(jnp.int32, (w, w), 0)
    c = lax.broadcasted_iota(jnp.int32, (w, w), 1)
    u_all = (r <= c).astype(BF16)
    fc = fc_ref[0, 0]
    parts = []
    off = jnp.zeros((nh, 1), F32)
    for kk in range(p_len // w):
        cs = _dot_f32_by_01(fc[:, kk * w:(kk + 1) * w], u_all) + off
        off = cs[:, w - 1:w]
        parts.append(cs)
    bias_c_all = off - jnp.concatenate(parts, axis=-1)
    bias_n_all = -gate_ref[0, frow:frow + nh, :]
    row = lax.broadcasted_iota(jnp.int32, (t, t), 0)
    col = lax.broadcasted_iota(jnp.int32, (t, t), 1)
    causal = col <= row
    neg_inf = jnp.float32(-jnp.inf)
    for h in range(nh):
        lanes = slice(h * hd, (h + 1) * hd)
        q = q_ref[0, :, lanes]
        s_c = _dot(q, kc_ref[0, 0, lanes, :].astype(BF16)) + bias_c_all[h:h + 1, :]
        s_n = _dot_nt(q, kn_ref[0, :, lanes]) + bias_n_all[h:h + 1, :]
        s_n = jnp.where(causal, s_n, neg_inf)
        m = jnp.maximum(jnp.max(s_c, axis=-1, keepdims=True), jnp.max(s_n, axis=-1, keepdims=True))
        p_c = jnp.exp(s_c - m)
        p_n = jnp.exp(s_n - m)
        l = jnp.sum(p_c, axis=-1, keepdims=True) + jnp.sum(p_n, axis=-1, keepdims=True)
        acc = (_dot_nt(p_c.astype(BF16), vc_ref[0, 0, lanes, :].astype(BF16))
               + _dot(p_n.astype(BF16), vn_ref[0, :, lanes]))
        o_ref[0, :, lanes] = (acc / l).astype(BF16)


def _fox_sample(q, k_new, v_new, k_cache_t, v_cache_t, logf_cache_t, layer, gate_rows, nh, hd, frow):
    b, t, df = q.shape
    p_len = k_cache_t.shape[3]
    ngr = gate_rows.shape[1]
    blk = lambda n, width: pl.BlockSpec((1, n, width), lambda i: (i, 0, 0))
    cache = lambda n: pl.BlockSpec((1, 1, n, p_len), lambda i: (layer, i, 0, 0))
    return pl.pallas_call(
        functools.partial(_fox_sample_kernel, nh=nh, hd=hd, frow=frow),
        out_shape=jax.ShapeDtypeStruct((b, t, df), BF16),
        grid=(b,),
        in_specs=[blk(t, df), blk(t, df), blk(t, df), cache(df), cache(df), cache(nh), blk(ngr, t)],
        out_specs=blk(t, df),
        compiler_params=_params("parallel"),
        name="fox_sample",
    )(q, k_new, v_new, k_cache_t, v_cache_t, logf_cache_t, gate_rows)


def _tail_kernel(x_ref, mod_ref, hm_ref, hf_ref, hg_ref, wo_ref, l1g_ref, l1b_ref, wg_ref, wu_ref, wd_ref,
                 l2g_ref, l2b_ref, o_ref, *, alpha, fchunk, hf_transposed):
    bb, ts, d = x_ref.shape
    dmm, dgg = hm_ref.shape[2], hg_ref.shape[2]
    dff = hf_ref.shape[1] if hf_transposed else hf_ref.shape[2]
    mod = mod_ref[...]
    wo_f = wo_ref[dmm:dmm + dff, :]
    f = wg_ref.shape[1]
    nsplit = TAIL_SPLIT if (bb == 1 and ts % (TAIL_SPLIT * 128) == 0) else 1
    tr = ts // nsplit
    n = bb * tr
    groups = [slice(g * tr, (g + 1) * tr) for g in range(nsplit)]

    mixes = []
    for rows in groups:
        mix_f = (_dot_tn(hf_ref[0, :, rows], wo_f) if hf_transposed
                 else _dot(hf_ref[:, rows, :].reshape(n, dff), wo_f))
        mixes.append(_dot(hm_ref[:, rows, :].reshape(n, dmm), wo_ref[0:dmm, :]) + mix_f
                     + _dot(hg_ref[:, rows, :].reshape(n, dgg), wo_ref[dmm + dff:dmm + dff + dgg, :]))
    x1s, h2s = [], []
    for rows, mix in zip(groups, mixes):
        x1 = _layer_norm(alpha * x_ref[:, rows, :] + (1.0 + mod[:, 2:3, :]) * mix.reshape(bb, tr, d),
                         l1g_ref[...], l1b_ref[...], LN_EPS)
        x1s.append(x1)
        h2s.append((x1 * (1.0 + mod[:, 4:5, :]) + mod[:, 3:4, :]).reshape(n, d).astype(BF16))
    ys = [jnp.zeros((n, d), F32) for _ in groups]
    for f0 in range(0, f, fchunk):
        gts = [_dot(h2, wg_ref[:, f0:f0 + fchunk]) for h2 in h2s]
        ups = [_dot(h2, wu_ref[:, f0:f0 + fchunk]) for h2 in h2s]
        acts = [(gt * _sigmoid(gt) * up).astype(BF16) for gt, up in zip(gts, ups)]
        ys = [y + _dot(act, wd_ref[f0:f0 + fchunk, :]) for y, act in zip(ys, acts)]
    for rows, x1, y in zip(groups, x1s, ys):
        o_ref[:, rows, :] = _layer_norm(alpha * x1 + (1.0 + mod[:, 5:6, :]) * y.reshape(bb, tr, d),
                                        l2g_ref[...], l2b_ref[...], LN_EPS)


def _ffn_chunk(f):
    for c in (2816, 1408, 704, 512, 384, 256, 128):
        if f % c == 0:
            return c
    return f


def _layer_tail(x, mod, hm, hf, hg, wo, l1g, l1b, wg, wu, wd, l2g, l2b, bb, ts, alpha, hf_transposed):
    b, s, d = x.shape
    f = wg.shape[1]
    row = lambda width: pl.BlockSpec((bb, ts, width), lambda i, j: (i, j, 0))
    vec = lambda a: a.reshape(1, d)
    if hf_transposed:
        assert bb == 1
        hf_spec = pl.BlockSpec((1, hf.shape[1], ts), lambda i, j: (i, 0, j))
    else:
        hf_spec = row(hf.shape[2])
    return pl.pallas_call(
        functools.partial(_tail_kernel, alpha=alpha, fchunk=_ffn_chunk(f), hf_transposed=hf_transposed),
        out_shape=jax.ShapeDtypeStruct((b, s, d), F32),
        grid=(b // bb, s // ts),
        in_specs=[row(d), pl.BlockSpec((bb, 6, d), lambda i, j: (i, 0, 0)),
                  row(hm.shape[2]), hf_spec, row(hg.shape[2]),
                  _resident(wo.shape), _resident((1, d)), _resident((1, d)),
                  _resident(wg.shape), _resident(wu.shape), _resident(wd.shape),
                  _resident((1, d)), _resident((1, d))],
        out_specs=row(d),
        compiler_params=_params("parallel", "parallel"),
        name="layer_tail",
    )(x, mod, hm, hf, hg, wo, vec(l1g), vec(l1b), wg, wu, wd, vec(l2g), vec(l2b))


def _tile(b, s):
    if s >= ROW_TILE:
        return 1, ROW_TILE
    bb = max(1, min(b, ROW_TILE // s))
    while b % bb:
        bb -= 1
    return bb, s


def _layer(x, mod, p, cache, dims, kv_prev=None):
    nh_m, nh_f, ng, hd = dims
    b, s, d = x.shape
    dm, df, dg = nh_m * hd, nh_f * hd, ng * hd
    bb, ts = _tile(b, s)
    alpha = p["alpha"]
    lc = min(s, p["gmlp_ws"].shape[-1])
    gm = (p["gmlp_ws"], p["gmlp_bs_t"], p["gmlp_ln_g"], p["gmlp_ln_b"])
    nrow = 2 * nh_m + nh_f

    if cache is None:
        gate_bias_row = jnp.pad(p["gate_bias"].reshape(1, nrow), ((0, 0), (0, LANE - nrow)))
        m_all, qa, ka, vte, k_t, v_t, hg, gates = _in_projection_prompt(
            x, mod, p["w_in"], gate_bias_row, gm, kv_prev, p["layer"], p["depth"], ts, 4 * dm, df, 2 * dg,
            hd ** -0.5 * LOG2E, ng, hd, lc, nh_f, 2 * nh_m)
    else:
        m_all, fq, fk, fv, fkb, fvb, hg, vn, gates = _in_projection_sample(
            x, mod, p["w_in"], gm, bb, ts, 4 * dm, df, 2 * dg, hd ** -0.5, ng, hd, lc)

    gates_t = jnp.transpose(gates[:, :, :nrow], (0, 2, 1))
    lm = s if cache is not None else min(MLSTM_PROMPT_CHUNK, s)
    gate_rows = _gate_prepass(gates_t, p["gate_bias"], nh_m, nh_f, lm)
    logf = jnp.transpose(gate_rows[:, 2 * nh_m:2 * nh_m + nh_f, :], (0, 2, 1))
    frow = 2 * nh_m + nh_f

    if cache is None:
        c0 = jnp.zeros((b, nh_m, hd, hd), F32)
        n0 = jnp.zeros((b, nh_m, hd, hd), F32)
        m0 = jnp.full((b, 1, nh_m), -jnp.inf, F32)
    else:
        k_c, v_c, logf_c, c_st, n_st, m_st = cache
        c0 = c_st
        n0 = jnp.broadcast_to(n_st[..., None], (b, nh_m, hd, hd))
        m0 = m_st.reshape(b, 1, nh_m)
    gate_cols = jnp.transpose(gate_rows[:, :2 * nh_m, :], (0, 2, 1))
    hm, c1, n1, m1 = _mlstm(m_all, gate_rows, gate_cols, c0, n0, m0, p["mlstm_norm_g"], nh_m, hd, lm)

    if cache is None:
        hf = _fox_prompt(qa, ka, vte, nh_f, hd)
    else:
        hf = _fox_sample(fq, fkb, fvb, k_c, v_c, logf_c, p["layer"], gate_rows, nh_f, hd, frow)

    x_out = _layer_tail(x, mod, hm, hf, hg, p["w_o"], p["ln1_g"], p["ln1_b"], p["w_gate"], p["w_up"],
                        p["w_down"], p["ln2_g"], p["ln2_b"], bb, ts, alpha, cache is None)
    if cache is None:
        new = (k_t, v_t, logf, c1, n1[..., 0], m1.reshape(b, nh_m))
    else:
        new = (fk.reshape(b, s, nh_f, hd), fv.reshape(b, s, nh_f, hd), logf,
               c1, n1[..., 0], m1.reshape(b, nh_m), vn)
    return x_out, new


def kernel(x_prompt, x_sample, cache_fox_k, cache_fox_v, cache_fox_logf, state_mlstm_C, state_mlstm_n, state_mlstm_m, c_prompt, c_sample, w_ada, b_ada, w_in, b_mlstm_i, b_mlstm_f, mlstm_norm_g, b_fox_f, gmlp_ln_g, gmlp_ln_b, gmlp_ws, gmlp_bs, w_o, ln1_g, ln1_b, w_gate, w_up, w_down, ln2_g, ln2_b):
    depth = w_in.shape[0]
    nh_m = b_mlstm_i.shape[1]
    nh_f = b_fox_f.shape[1]
    ng = gmlp_ws.shape[1]
    hd = cache_fox_k.shape[-1]
    dm, df, dg = nh_m * hd, nh_f * hd, ng * hd
    d = x_prompt.shape[-1]
    bp = x_prompt.shape[0]
    alpha = (2 * depth) ** 0.25

    o_mi = 4 * dm
    o_fq = o_mi + 2 * nh_m
    o_ff = o_fq + 3 * df
    o_gu = o_ff + nh_f
    npad = LANE - (2 * nh_m + nh_f)

    bs, p_len = cache_fox_k.shape[1], cache_fox_k.shape[2]
    cache_kt = jnp.transpose(cache_fox_k, (0, 1, 3, 4, 2)).reshape(depth, bs, df, p_len)
    cache_vt = jnp.transpose(cache_fox_v, (0, 1, 3, 4, 2)).reshape(depth, bs, df, p_len)
    cache_ft = jnp.transpose(cache_fox_logf, (0, 1, 3, 2))

    xp, xs = x_prompt, x_sample
    c_all = jnp.concatenate([c_prompt, c_sample], axis=0)
    new_p, new_s = [], []
    kv_prev = None
    for l in range(depth):
        wl = w_in[l]
        w_all = jnp.concatenate(
            [wl[:, :o_mi], wl[:, o_fq:o_ff], wl[:, o_gu:o_gu + 2 * dg], wl[:, o_mi:o_fq], wl[:, o_ff:o_gu],
             jnp.zeros((d, npad), wl.dtype)], axis=1).astype(BF16)
        p = dict(
            alpha=alpha, w_in=w_all, layer=l, depth=depth,
            gate_bias=jnp.concatenate([b_mlstm_i[l], b_mlstm_f[l], b_fox_f[l]]).reshape(-1, 1),
            mlstm_norm_g=mlstm_norm_g[l], gmlp_ln_g=gmlp_ln_g[l], gmlp_ln_b=gmlp_ln_b[l],
            gmlp_ws=gmlp_ws[l], gmlp_bs_t=jnp.transpose(gmlp_bs[l]),
            w_o=w_o[l].astype(BF16), ln1_g=ln1_g[l], ln1_b=ln1_b[l],
            w_gate=w_gate[l].astype(BF16), w_up=w_up[l].astype(BF16), w_down=w_down[l].astype(BF16),
            ln2_g=ln2_g[l], ln2_b=ln2_b[l])
        mod = _modulation(c_all, w_ada, b_ada, l).reshape(c_all.shape[0], 6, d)
        xp, st_p = _layer(xp, mod[:bp], p, None, (nh_m, nh_f, ng, hd), kv_prev)
        kv_prev = st_p[:2]
        xs, st_s = _layer(xs, mod[bp:], p, (cache_kt, cache_vt, cache_ft,
                                            state_mlstm_C[l], state_mlstm_n[l], state_mlstm_m[l]),
                          (nh_m, nh_f, ng, hd))
        new_p.append(st_p)
        new_s.append(st_s)

    def stk(lst, i):
        return jnp.stack([e[i] for e in lst], axis=0)

    def heads_last(kv_t):
        sp = kv_t.shape[-1]
        return jnp.transpose(kv_t.reshape(depth, bp, nh_f, hd, sp), (0, 1, 4, 2, 3))

    return (xp, xs,
            heads_last(kv_prev[0]), heads_last(kv_prev[1]), stk(new_p, 2), stk(new_p, 3), stk(new_p, 4), stk(new_p, 5),
            stk(new_s, 0), stk(new_s, 1), stk(new_s, 2), stk(new_s, 3), stk(new_s, 4), stk(new_s, 5),
            stk(new_s, 6))
```

```python
import functools

import jax
import jax.numpy as jnp
from jax import lax
from jax.experimental import pallas as pl
from jax.experimental.pallas import tpu as pltpu

F32 = jnp.float32
BF16 = jnp.bfloat16

LOG2E = 1.4426950408889634
LN_EPS = 1e-5
HN_EPS = 1e-6
MLSTM_PROMPT_CHUNK = 256
MLSTM_BATCH_ROWS = 4
ROW_TILE = 512
TAIL_SPLIT = 2
FOX_TILE = 512
FOX_SKEW = 1
FOX_ONES_ROWS = 16
LANE = 128
MXU_COLS = 256
FFN_CHUNK_MAX = 4096
VMEM_LIMIT = 56 * 1024 * 1024


def _sigmoid(x):
    return 1.0 / (1.0 + jnp.exp(-x))


def _log_sigmoid(x):
    return jnp.minimum(x, 0.0) - jnp.log1p(jnp.exp(-jnp.abs(x)))


def _dot(a, b):
    return jnp.dot(a, b, preferred_element_type=F32)


def _dot_nt(a, b):
    return lax.dot_general(a, b, (((1,), (1,)), ((), ())), preferred_element_type=F32)


def _dot_tn(a, b):
    return lax.dot_general(a, b, (((0,), (0,)), ((), ())), preferred_element_type=F32)


def _dot_f32_by_01(x, u):
    hi = x.astype(BF16)
    r1 = x - hi.astype(F32)
    mid = r1.astype(BF16)
    lo = (r1 - mid.astype(F32)).astype(BF16)
    return _dot(hi, u) + _dot(mid, u) + _dot(lo, u)


def _layer_norm(x, g, b, eps):
    mu = jnp.mean(x, axis=-1, keepdims=True)
    xc = x - mu
    var = jnp.mean(xc * xc, axis=-1, keepdims=True)
    return xc * lax.rsqrt(var + eps) * g + b


def _params(*sem):
    return pltpu.CompilerParams(dimension_semantics=sem, vmem_limit_bytes=VMEM_LIMIT)


def _resident(shape):
    nd = len(shape)
    return pl.BlockSpec(shape, lambda *_: (0,) * nd, pipeline_mode=pl.Buffered(1))


def _mod_kernel(c_ref, w_ref, b_ref, o_ref):
    c = c_ref[...]
    s = (c * _sigmoid(c)).astype(BF16)
    o_ref[...] = _dot(s, w_ref[0].astype(BF16)) + b_ref[0]


def _modulation(c, w_ada, b_ada, layer):
    nb, d = c.shape
    depth, _, n = w_ada.shape
    tn = n // 4 if (n // 4) % LANE == 0 else n
    return pl.pallas_call(
        _mod_kernel,
        out_shape=jax.ShapeDtypeStruct((nb, n), F32),
        grid=(n // tn,),
        in_specs=[pl.BlockSpec((nb, d), lambda j: (0, 0)),
                  pl.BlockSpec((1, d, tn), lambda j: (layer, 0, j)),
                  pl.BlockSpec((1, 1, tn), lambda j: (layer, 0, j))],
        out_specs=pl.BlockSpec((nb, tn), lambda j: (0, j)),
        compiler_params=_params("parallel"),
        name="adaln_mod",
    )(c, w_ada, b_ada.reshape(depth, 1, n))


def _modulated(x_ref, mod_ref):
    bb, ts, d = x_ref.shape
    mod = mod_ref[...]
    h = x_ref[...] * (1.0 + mod[:, 1:2, :]) + mod[:, 0:1, :]
    return h.reshape(bb * ts, d).astype(BF16)


def _spatial_gate(gall, gw_ref, gbs_ref, lng_ref, lnb_ref, hg_ref, *, ng, hd, lc):
    bb, ts, dg = hg_ref.shape
    n = bb * ts
    gu = gall[:, :dg]
    vn = _layer_norm(gall[:, dg:], lng_ref[...], lnb_ref[...], LN_EPS)
    row = lax.broadcasted_iota(jnp.int32, (lc, lc), 0)
    col = lax.broadcasted_iota(jnp.int32, (lc, lc), 1)
    tri = col <= row
    per_row = ts // lc
    zs = {}
    for g in range(ng):
        lanes = slice(g * hd, (g + 1) * hd)
        wg = jnp.where(tri, gw_ref[g, 0:lc, 0:lc], 0.0).astype(BF16)
        for ci in range(n // lc):
            zs[g, ci] = _dot(wg, vn[ci * lc:(ci + 1) * lc, lanes].astype(BF16))
    for g in range(ng):
        lanes = slice(g * hd, (g + 1) * hd)
        bcol = gbs_ref[0:lc, g:g + 1]
        for ci in range(n // lc):
            rows = slice(ci * lc, (ci + 1) * lc)
            bi, ri = ci // per_row, ci % per_row
            hg_ref[bi, ri * lc:(ri + 1) * lc, lanes] = (gu[rows, lanes] * (zs[g, ci] + bcol)).astype(BF16)
    return vn


def _inproj_sample_kernel(x_ref, mod_ref, w_ref, gw_ref, gbs_ref, lng_ref, lnb_ref,
                          m_ref, fq_ref, fk_ref, fv_ref, fkb_ref, fvb_ref, hg_ref, vn_ref, gate_ref,
                          *, dm4, df, dg2, q_scale, ng, hd, lc):
    bb, ts, _ = x_ref.shape
    h = _modulated(x_ref, mod_ref)

    def seg(c0, width):
        return _dot(h, w_ref[:, c0:c0 + width])

    m_ref[...] = seg(0, dm4).astype(BF16).reshape(bb, ts, dm4)
    c0 = dm4
    fq_ref[...] = (seg(c0, df) * q_scale).astype(BF16).reshape(bb, ts, df)
    fk = seg(c0 + df, df)
    fk_ref[...] = fk.reshape(bb, ts, df)
    fkb_ref[...] = fk.astype(BF16).reshape(bb, ts, df)
    fv = seg(c0 + 2 * df, df)
    fv_ref[...] = fv.reshape(bb, ts, df)
    fvb_ref[...] = fv.astype(BF16).reshape(bb, ts, df)
    c0 += 3 * df
    gate_ref[...] = seg(c0 + dg2, LANE).reshape(bb, ts, LANE)
    vn = _spatial_gate(seg(c0, dg2), gw_ref, gbs_ref, lng_ref, lnb_ref, hg_ref, ng=ng, hd=hd, lc=lc)
    vn_ref[...] = vn.reshape(bb, ts, dg2 // 2)


def _inproj_prompt_kernel(x_ref, mod_ref, w_ref, gbias_ref, tril_ref, gw_ref, gbs_ref, lng_ref, lnb_ref, *rest,
                          dm4, df, dg2, q_scale, ng, hd, lc, nh, fcol, aliased):
    if aliased:
        rest = rest[2:]
    m_ref, qa_ref, ka_ref, vte_ref, kt_ref, vt_ref, hg_ref, gate_ref, carry = rest
    _, ts, _ = x_ref.shape

    @pl.when(pl.program_id(1) == 0)
    def _():
        carry[...] = jnp.zeros_like(carry)

    h = _modulated(x_ref, mod_ref)

    def seg(c0, width):
        return _dot(h, w_ref[:, c0:c0 + width])

    c_f = dm4
    c_g = dm4 + 3 * df
    gt = seg(c_g + dg2, LANE)
    gate_ref[0] = gt

    logf = _log_sigmoid(gt + gbias_ref[...])
    lower = tril_ref[...]
    hi = logf.astype(BF16)
    r1 = logf - hi.astype(F32)
    mid = r1.astype(BF16)
    lo = (r1 - mid.astype(F32)).astype(BF16)
    fcum = _dot(lower, hi) + _dot(lower, mid) + _dot(lower, lo) + carry[...]
    carry[...] = fcum[ts - 1:ts, :]
    nf = fcum * (-LOG2E)

    fk = seg(c_f + df, df)
    fv = seg(c_f + 2 * df, df)
    kt_ref[0, 0] = jnp.transpose(fk)
    vt = jnp.transpose(fv)
    vt_ref[0, 0] = vt
    if not aliased and kt_ref.shape[0] > 1:
        kt_ref[1:, 0] = jnp.zeros((kt_ref.shape[0] - 1,) + kt_ref.shape[2:], F32)
        vt_ref[1:, 0] = jnp.zeros((vt_ref.shape[0] - 1,) + vt_ref.shape[2:], F32)
    ones = jnp.ones((FOX_ONES_ROWS, ts), F32)
    v_parts = []
    for hh in range(nh):
        v_parts += [vt[hh * hd:(hh + 1) * hd, :], ones]
    vte_ref[0] = jnp.concatenate(v_parts, axis=0).astype(BF16)

    fq = (seg(c_f, df) * q_scale).astype(BF16)
    lane = lax.broadcasted_iota(jnp.int32, (ts, hd), 1)
    ones3 = (lane < 3).astype(BF16)
    fkb = fk.astype(BF16)
    q_parts, k_parts = [], []
    for hh in range(nh):
        lanes = slice(hh * hd, (hh + 1) * hd)
        x = nf[:, fcol + hh:fcol + hh + 1]
        xh = x.astype(BF16).astype(F32)
        x1 = x - xh
        xm = x1.astype(BF16).astype(F32)
        extra = jnp.where(lane == 0, xh, jnp.where(lane == 1, xm, jnp.where(lane == 2, x1 - xm, 0.0)))
        q_parts += [fq[:, lanes], ones3]
        k_parts += [fkb[:, lanes], extra.astype(BF16)]
    gall = seg(c_g, dg2)
    ka_ref[0] = jnp.concatenate(k_parts, axis=-1)
    qa_ref[0] = jnp.concatenate(q_parts, axis=-1)
    m_lo = seg(0, dm4 // 2).astype(BF16)
    _spatial_gate(gall, gw_ref, gbs_ref, lng_ref, lnb_ref, hg_ref, ng=ng, hd=hd, lc=lc)
    m_hi = seg(dm4 // 2, dm4 // 2).astype(BF16)
    m_ref[0, :, 0:dm4 // 2] = m_lo
    m_ref[0, :, dm4 // 2:dm4] = m_hi


def _gmlp_specs(gw, gbs_t, dg):
    return [_resident(gw.shape), _resident(gbs_t.shape), _resident((1, dg)), _resident((1, dg))]


def _in_projection_sample(x, mod, w_all, gm, bb, ts, dm4, df, dg2, q_scale, ng, hd, lc):
    b, s, d = x.shape
    dg = dg2 // 2
    gw, gbs_t, lng, lnb = gm
    row = lambda width: pl.BlockSpec((bb, ts, width), lambda i, j: (i, j, 0))
    shp = lambda width, dt: jax.ShapeDtypeStruct((b, s, width), dt)
    return pl.pallas_call(
        functools.partial(_inproj_sample_kernel, dm4=dm4, df=df, dg2=dg2, q_scale=q_scale, ng=ng, hd=hd, lc=lc),
        out_shape=[shp(dm4, BF16), shp(df, BF16), shp(df, F32), shp(df, F32), shp(df, BF16), shp(df, BF16),
                   shp(dg, BF16), shp(dg, F32), shp(LANE, F32)],
        grid=(b // bb, s // ts),
        in_specs=[row(d), pl.BlockSpec((bb, 6, d), lambda i, j: (i, 0, 0)), _resident(w_all.shape)]
        + _gmlp_specs(gw, gbs_t, dg),
        out_specs=[row(dm4), row(df), row(df), row(df), row(df), row(df), row(dg), row(dg), row(LANE)],
        compiler_params=_params("parallel", "parallel"),
        name="in_proj_sample",
    )(x, mod, w_all, gw, gbs_t, lng.reshape(1, dg), lnb.reshape(1, dg))


def _in_projection_prompt(x, mod, w_all, gate_bias_row, gm, kv_prev, layer, depth, ts, dm4, df, dg2, q_scale,
                          ng, hd, lc, nh, fcol):
    b, s, d = x.shape
    dg = dg2 // 2
    dv = nh * (hd + FOX_ONES_ROWS)
    gw, gbs_t, lng, lnb = gm
    aliased = kv_prev is not None
    row = lambda width: pl.BlockSpec((1, ts, width), lambda i, j: (i, j, 0))
    shp = lambda width, dt: jax.ShapeDtypeStruct((b, s, width), dt)
    kv_shape = jax.ShapeDtypeStruct((depth, b, df, s), F32)
    if aliased:
        kv_spec = pl.BlockSpec((1, 1, df, ts), lambda i, j: (layer, i, 0, j))
    else:
        assert layer == 0
        kv_spec = pl.BlockSpec((depth, 1, df, ts), lambda i, j: (0, i, 0, j))
    in_specs = ([row(d), pl.BlockSpec((1, 6, d), lambda i, j: (i, 0, 0)), _resident(w_all.shape),
                 _resident((1, LANE)), _resident((ts, ts))] + _gmlp_specs(gw, gbs_t, dg))
    tril = jnp.tril(jnp.ones((ts, ts), BF16))
    args = [x, mod, w_all, gate_bias_row, tril, gw, gbs_t, lng.reshape(1, dg), lnb.reshape(1, dg)]
    aliases = {}
    if aliased:
        in_specs += [pl.BlockSpec(memory_space=pl.ANY), pl.BlockSpec(memory_space=pl.ANY)]
        aliases = {len(args): 4, len(args) + 1: 5}
        args += list(kv_prev)
    return pl.pallas_call(
        functools.partial(_inproj_prompt_kernel, dm4=dm4, df=df, dg2=dg2, q_scale=q_scale, ng=ng, hd=hd, lc=lc,
                          nh=nh, fcol=fcol, aliased=aliased),
        out_shape=[shp(dm4, BF16), shp(2 * df, BF16), shp(2 * df, BF16),
                   jax.ShapeDtypeStruct((b, dv, s), BF16), kv_shape, kv_shape, shp(dg, BF16), shp(LANE, F32)],
        grid=(b, s // ts),
        in_specs=in_specs,
        out_specs=[row(dm4), row(2 * df), row(2 * df), pl.BlockSpec((1, dv, ts), lambda i, j: (i, 0, j)),
                   kv_spec, kv_spec, row(dg), row(LANE)],
        scratch_shapes=[pltpu.VMEM((1, LANE), F32)],
        input_output_aliases=aliases,
        compiler_params=_params("parallel", "arbitrary"),
        name="in_proj_prompt",
    )(*args)


def _gates_kernel(g_ref, bias_ref, o_ref, *, hm, hf, lm):
    s = g_ref.shape[2]
    w = min(LANE, s)
    x = g_ref[0] + bias_ref[...]
    ls = _log_sigmoid(x)
    r = lax.broadcasted_iota(jnp.int32, (w, w), 0)
    c = lax.broadcasted_iota(jnp.int32, (w, w), 1)
    upper = r <= c
    u_all = upper.astype(BF16)
    u_chunk = jnp.logical_and(upper, (r // lm) == (c // lm)).astype(BF16)
    off = jnp.zeros((hf, 1), F32)
    b_off = jnp.zeros((hm, 1), F32)
    for k in range(s // w):
        cols = slice(k * w, (k + 1) * w)
        li = x[0:hm, cols]
        if (k * w) % lm == 0:
            b_off = jnp.zeros((hm, 1), F32)
        b = _dot_f32_by_01(ls[hm:2 * hm, cols], u_chunk) + b_off
        if lm > w:
            b_off = b[:, w - 1:w]
        logf = ls[2 * hm:2 * hm + hf, cols]
        fcum = _dot_f32_by_01(logf, u_all) + off
        off = fcum[:, w - 1:w]
        o_ref[0, 0:hm, cols] = li - b
        o_ref[0, hm:2 * hm, cols] = b
        o_ref[0, 2 * hm:2 * hm + hf, cols] = logf
        o_ref[0, 2 * hm + hf:2 * hm + 2 * hf, cols] = fcum


def _gate_prepass(gates_t, bias_col, hm, hf, lm):
    b, nr, s = gates_t.shape
    nout = 2 * hm + 2 * hf
    return pl.pallas_call(
        functools.partial(_gates_kernel, hm=hm, hf=hf, lm=lm),
        out_shape=jax.ShapeDtypeStruct((b, nout, s), F32),
        grid=(b,),
        in_specs=[pl.BlockSpec((1, nr, s), lambda i: (i, 0, 0)), pl.BlockSpec((nr, 1), lambda i: (0, 0))],
        out_specs=pl.BlockSpec((1, nout, s), lambda i: (i, 0, 0)),
        compiler_params=_params("parallel"),
        name="gate_prepass",
    )(gates_t, bias_col)


def _mlstm_kernel(qkvo_ref, gate_ref, gcol_ref, c0_ref, n0_ref, m0_ref, ng_ref, hm_ref, c_ref, n_ref, m_ref,
                  *, nh, hd, lc):
    @pl.when(pl.program_id(1) == 0)
    def _():
        c_ref[...] = c0_ref[...]
        n_ref[...] = n0_ref[...]
        m_ref[...] = m0_ref[...]

    nb = qkvo_ref.shape[0]
    rows_per_step = qkvo_ref.shape[1]
    dm = nh * hd
    ksc = hd ** -0.5
    row = lax.broadcasted_iota(jnp.int32, (lc, lc), 0)
    col = lax.broadcasted_iota(jnp.int32, (lc, lc), 1)
    tri = col <= row
    neg_inf = jnp.float32(-jnp.inf)
    ones_v = jnp.ones((lc, hd), BF16)
    avg = jnp.full((hd, hd), 1.0 / hd, BF16)

    chains = [(bi, h) for bi in range(nb) for h in range(nh)]
    for ci in range(rows_per_step // lc):
        rows = slice(ci * lc, (ci + 1) * lc)
        m_olds = [m_ref[bi] for bi in range(nb)]
        st = {}
        for bi, h in chains:
            q = qkvo_ref[bi, rows, h * hd:(h + 1) * hd]
            k = qkvo_ref[bi, rows, dm + h * hd:dm + (h + 1) * hd]
            a_row = gate_ref[bi, h:h + 1, rows]
            m0 = m_olds[bi][:, h:h + 1]
            c_old = c_ref[bi, h]
            n_old = n_ref[bi, h]
            a_full = jnp.broadcast_to(a_row, (lc, lc))
            g_col = jnp.maximum(jnp.max(jnp.where(tri, a_full, neg_inf), axis=-1, keepdims=True), m0)
            st[bi, h] = dict(k=k, m0=m0, c_old=c_old, n_old=n_old, a_full=a_full, g_col=g_col,
                             qk=_dot_nt(q, k), qc=_dot(q, c_old.astype(BF16)), qn=_dot(q, n_old.astype(BF16)))
        for bi, h in chains:
            c = st[bi, h]
            v = qkvo_ref[bi, rows, 2 * dm + h * hd:2 * dm + (h + 1) * hd]
            g_b = jnp.broadcast_to(c["g_col"], (lc, max(lc, hd)))
            a_b = jnp.broadcast_to(gcol_ref[bi, rows, h:h + 1], (lc, hd))
            b_b = jnp.broadcast_to(gcol_ref[bi, rows, nh + h:nh + h + 1], (lc, hd))
            g_h = g_b[:, :hd]
            w_intra = jnp.exp(jnp.where(tri, c["a_full"] - g_b[:, :lc], neg_inf))
            c["sc"] = (c["qk"] * (w_intra * ksc)).astype(BF16)
            c["w_inter"] = jnp.exp(c["m0"] - g_h)
            c["floor"] = jnp.exp(-(b_b + g_h))
            c["g_end"] = c["g_col"][lc - 1:lc, :]
            w_end = jnp.exp(a_b - c["g_end"]) * ksc
            c["w_end"] = w_end.astype(BF16)
            c["vw"] = v * c["w_end"]
            c["v"] = v
        m_new = [[] for _ in range(nb)]
        for bi, h in chains:
            c = st[bi, h]
            decay = jnp.exp(c["m0"] - c["g_end"])
            c_ref[bi, h] = decay * c["c_old"] + _dot_tn(c["k"], c["vw"])
            n_ref[bi, h] = decay * c["n_old"] + _dot_tn(c["k"], c["w_end"])
            b_row = gate_ref[bi, nh + h:nh + h + 1, rows]
            m_new[bi].append(b_row[:, lc - 1:lc] + c["g_end"])
        for bi in range(nb):
            m_ref[bi] = jnp.concatenate(m_new[bi], axis=-1)
        for bi, h in chains:
            c = st[bi, h]
            num = _dot(c["sc"], c["v"]) + c["qc"] * c["w_inter"]
            den = _dot(c["sc"], ones_v) + c["qn"] * c["w_inter"]
            c["hh"] = num / jnp.maximum(jnp.abs(den), c["floor"])
        for bi, h in chains:
            c = st[bi, h]
            c["hc"] = c["hh"] - _dot(c["hh"].astype(BF16), avg)
        for bi, h in chains:
            c = st[bi, h]
            c["var"] = _dot((c["hc"] * c["hc"]).astype(BF16), avg)
        for bi, h in chains:
            c = st[bi, h]
            lanes = slice(h * hd, (h + 1) * hd)
            o = qkvo_ref[bi, rows, 3 * dm + h * hd:3 * dm + (h + 1) * hd]
            out = c["hc"] * lax.rsqrt(c["var"] + HN_EPS) * ng_ref[0:1, lanes] * _sigmoid(o.astype(F32))
            hm_ref[bi, rows, lanes] = out.astype(BF16)


def _mlstm(qkvo, gate_rows, gate_cols, c0, n0_rep, m0, norm_g, nh, hd, lc):
    b, s, d4 = qkvo.shape
    dm = nh * hd
    rows = max(lc, min(LANE, s))
    ngr = gate_rows.shape[1]
    ngc = gate_cols.shape[2]
    nb = MLSTM_BATCH_ROWS if b % MLSTM_BATCH_ROWS == 0 else 1
    state = pl.BlockSpec((nb, nh, hd, hd), lambda i, j: (i, 0, 0, 0))
    mspec = pl.BlockSpec((nb, 1, nh), lambda i, j: (i, 0, 0))
    return pl.pallas_call(
        functools.partial(_mlstm_kernel, nh=nh, hd=hd, lc=lc),
        out_shape=[jax.ShapeDtypeStruct((b, s, dm), BF16),
                   jax.ShapeDtypeStruct((b, nh, hd, hd), F32),
                   jax.ShapeDtypeStruct((b, nh, hd, hd), F32),
                   jax.ShapeDtypeStruct((b, 1, nh), F32)],
        grid=(b // nb, s // rows),
        in_specs=[pl.BlockSpec((nb, rows, d4), lambda i, j: (i, j, 0)),
                  pl.BlockSpec((nb, ngr, rows), lambda i, j: (i, 0, j)),
                  pl.BlockSpec((nb, rows, ngc), lambda i, j: (i, j, 0)),
                  state, state, mspec,
                  pl.BlockSpec((1, dm), lambda i, j: (0, 0))],
        out_specs=[pl.BlockSpec((nb, rows, dm), lambda i, j: (i, j, 0)), state, state, mspec],
        compiler_params=_params("parallel", "arbitrary"),
        name="mlstm_scan",
    )(qkvo, gate_rows, gate_cols, c0, n0_rep, m0, norm_g.reshape(1, dm))


def _fox_prompt_kernel(qa_ref, ka_ref, vt_ref, o_ref, m_sc, acc_sc, *, nh, hd, t):
    i = pl.program_id(1)
    q0 = pl.multiple_of(i * t, t)
    w = 2 * hd
    hv = hd + FOX_ONES_ROWS
    neg_inf = jnp.float32(-jnp.inf)

    th = t // 2
    q1 = pl.multiple_of(q0 + th, th)
    vis_a = lax.broadcasted_iota(jnp.int32, (th, t), 0) <= lax.broadcasted_iota(jnp.int32, (th, t), 1)
    vis_b = lax.broadcasted_iota(jnp.int32, (th, th), 0) <= lax.broadcasted_iota(jnp.int32, (th, th), 1)
    s_a = [_dot_nt(ka_ref[0, pl.ds(q0, th), h * w:(h + 1) * w], qa_ref[0, :, h * w:(h + 1) * w])
           for h in range(nh)]
    s_b = [_dot_nt(ka_ref[0, pl.ds(q1, th), h * w:(h + 1) * w], qa_ref[0, th:, h * w:(h + 1) * w])
           for h in range(nh)]
    ms, p_a, p_b = [], [], []
    for h in range(nh):
        sa = jnp.where(vis_a, s_a[h], neg_inf)
        sb = jnp.where(vis_b, s_b[h], neg_inf)
        m_a = jnp.max(sa, axis=0, keepdims=True)
        m_hi = jnp.maximum(m_a[:, th:], jnp.max(sb, axis=0, keepdims=True))
        m = jnp.concatenate([m_a[:, :th], m_hi], axis=1)
        ms.append(m)
        p_a.append(jnp.exp2(sa - m).astype(BF16))
        p_b.append(jnp.exp2(sb - m_hi).astype(BF16))
    m_sc[...] = jnp.concatenate(ms, axis=0)
    pv_a = [_dot(vt_ref[0, h * hv:(h + 1) * hv, pl.ds(q0, th)], p_a[h]) for h in range(nh)]
    pv_b = [_dot(vt_ref[0, h * hv:(h + 1) * hv, pl.ds(q1, th)], p_b[h]) for h in range(nh)]
    for h in range(nh):
        acc_sc[h * hv:(h + 1) * hv, :th] = pv_a[h][:, :th]
        acc_sc[h * hv:(h + 1) * hv, th:] = pv_a[h][:, th:] + pv_b[h]

    def body(j, carry):
        k0 = pl.multiple_of(j * t, t)
        m_all = m_sc[...]
        ss, ms, ps, alphas = {}, {}, {}, {}
        for step in range(nh + FOX_SKEW + 1):
            h_a, h_b, h_c = step, step - FOX_SKEW, step - FOX_SKEW - 1
            if h_a < nh:
                ss[h_a] = _dot_nt(ka_ref[0, pl.ds(k0, t), h_a * w:(h_a + 1) * w], qa_ref[0, :, h_a * w:(h_a + 1) * w])
            if 0 <= h_b < nh:
                m_old = m_all[h_b:h_b + 1, :]
                m_new = jnp.maximum(m_old, jnp.max(ss[h_b], axis=0, keepdims=True))
                alphas[h_b] = jnp.exp2(m_old - m_new)
                ps[h_b] = jnp.exp2(ss.pop(h_b) - m_new).astype(BF16)
                ms[h_b] = m_new
            if 0 <= h_c < nh:
                rows = slice(h_c * hv, (h_c + 1) * hv)
                pv = _dot(vt_ref[0, rows, pl.ds(k0, t)], ps.pop(h_c))
                acc_sc[rows, :] = alphas[h_c] * acc_sc[rows, :] + pv
        m_sc[...] = jnp.concatenate([ms[h] for h in range(nh)], axis=0)
        return carry

    lax.fori_loop(0, i, body, 0)
    for h in range(nh):
        o_ref[0, h * hd:(h + 1) * hd, :] = (acc_sc[h * hv:h * hv + hd, :]
                                            / acc_sc[h * hv + hd:h * hv + hd + 1, :]).astype(BF16)


def _fox_prompt(qa, ka, vt, nh, hd):
    b, s, w2 = qa.shape
    df = nh * hd
    dv = vt.shape[1]
    t = min(FOX_TILE, s)
    return pl.pallas_call(
        functools.partial(_fox_prompt_kernel, nh=nh, hd=hd, t=t),
        out_shape=jax.ShapeDtypeStruct((b, df, s), BF16),
        grid=(b, s // t),
        in_specs=[pl.BlockSpec((1, t, w2), lambda i, j: (i, j, 0)),
                  pl.BlockSpec((1, s, w2), lambda i, j: (i, 0, 0)),
                  pl.BlockSpec((1, dv, s), lambda i, j: (i, 0, 0))],
        out_specs=pl.BlockSpec((1, df, t), lambda i, j: (i, 0, j)),
        scratch_shapes=[pltpu.VMEM((nh, t), F32), pltpu.VMEM((dv, t), F32)],
        compiler_params=_params("parallel", "arbitrary"),
        name="fox_prompt",
    )(qa, ka, vt)


def _fox_sample_kernel(q_ref, kn_ref, vn_ref, kc_ref, vc_ref, fc_ref, gate_ref, o_ref, *, nh, hd, frow):
    t = q_ref.shape[1]
    p_len = kc_ref.shape[3]
    w = min(LANE, p_len)
    r = lax.broadcasted_iota(jnp.int32, (w, w), 0)
    c = lax.broadcasted_iota(jnp.int32, (w, w), 1)
    u_all = (r <= c).astype(BF16)
    fc = fc_ref[0, 0]
    parts = []
    off = jnp.zeros((nh, 1), F32)
    for kk in range(p_len // w):
        cs = _dot_f32_by_01(fc[:, kk * w:(kk + 1) * w], u_all) + off
        off = cs[:, w - 1:w]
        parts.append(cs)
    bias_c_all = off - jnp.concatenate(parts, axis=-1)
    bias_n_all = -gate_ref[0, frow:frow + nh, :]
    row = lax.broadcasted_iota(jnp.int32, (t, t), 0)
    col = lax.broadcasted_iota(jnp.int32, (t, t), 1)
    causal = col <= row
    neg_inf = jnp.float32(-jnp.inf)
    for h in range(nh):
        lanes = slice(h * hd, (h + 1) * hd)
        q = q_ref[0, :, lanes]
        s_c = _dot(q, kc_ref[0, 0, lanes, :].astype(BF16)) + bias_c_all[h:h + 1, :]
        s_n = _dot_nt(q, kn_ref[0, :, lanes]) + bias_n_all[h:h + 1, :]
        s_n = jnp.where(causal, s_n, neg_inf)
        m = jnp.maximum(jnp.max(s_c, axis=-1, keepdims=True), jnp.max(s_n, axis=-1, keepdims=True))
        p_c = jnp.exp(s_c - m)
        p_n = jnp.exp(s_n - m)
        l = jnp.sum(p_c, axis=-1, keepdims=True) + jnp.sum(p_n, axis=-1, keepdims=True)
        acc = (_dot_nt(p_c.astype(BF16), vc_ref[0, 0, lanes, :].astype(BF16))
               + _dot(p_n.astype(BF16), vn_ref[0, :, lanes]))
        o_ref[0, :, lanes] = (acc / l).astype(BF16)


def _fox_sample(q, k_new, v_new, k_cache_t, v_cache_t, logf_cache_t, layer, gate_rows, nh, hd, frow):
    b, t, df = q.shape
    p_len = k_cache_t.shape[3]
    ngr = gate_rows.shape[1]
    blk = lambda n, width: pl.BlockSpec((1, n, width), lambda i: (i, 0, 0))
    cache = lambda n: pl.BlockSpec((1, 1, n, p_len), lambda i: (layer, i, 0, 0))
    return pl.pallas_call(
        functools.partial(_fox_sample_kernel, nh=nh, hd=hd, frow=frow),
        out_shape=jax.ShapeDtypeStruct((b, t, df), BF16),
        grid=(b,),
        in_specs=[blk(t, df), blk(t, df), blk(t, df), cache(df), cache(df), cache(nh), blk(ngr, t)],
        out_specs=blk(t, df),
        compiler_params=_params("parallel"),
        name="fox_sample",
    )(q, k_new, v_new, k_cache_t, v_cache_t, logf_cache_t, gate_rows)


def _tail_kernel(x_ref, mod_ref, hm_ref, hf_ref, hg_ref, wo_ref, l1g_ref, l1b_ref, wg_ref, wu_ref, wd_ref,
                 l2g_ref, l2b_ref, o_ref, *, alpha, fchunk, hf_transposed):
    bb, ts, d = x_ref.shape
    dmm, dgg = hm_ref.shape[2], hg_ref.shape[2]
    dff = hf_ref.shape[1] if hf_transposed else hf_ref.shape[2]
    mod = mod_ref[...]
    wo_f = wo_ref[dmm:dmm + dff, :]
    f = wg_ref.shape[1]
    nsplit = TAIL_SPLIT if (bb == 1 and ts % (TAIL_SPLIT * 128) == 0) else 1
    tr = ts // nsplit
    n = bb * tr
    groups = [slice(g * tr, (g + 1) * tr) for g in range(nsplit)]

    mixes = []
    for rows in groups:
        mix_f = (_dot_tn(hf_ref[0, :, rows], wo_f) if hf_transposed
                 else _dot(hf_ref[:, rows, :].reshape(n, dff), wo_f))
        mixes.append(_dot(hm_ref[:, rows, :].reshape(n, dmm), wo_ref[0:dmm, :]) + mix_f
                     + _dot(hg_ref[:, rows, :].reshape(n, dgg), wo_ref[dmm + dff:dmm + dff + dgg, :]))
    x1s, h2s = [], []
    for rows, mix in zip(groups, mixes):
        x1 = _layer_norm(alpha * x_ref[:, rows, :] + (1.0 + mod[:, 2:3, :]) * mix.reshape(bb, tr, d),
                         l1g_ref[...], l1b_ref[...], LN_EPS)
        x1s.append(x1)
        h2s.append((x1 * (1.0 + mod[:, 4:5, :]) + mod[:, 3:4, :]).reshape(n, d).astype(BF16))
    ys = [jnp.zeros((n, d), F32) for _ in groups]
    for f0 in range(0, f, fchunk):
        gts = [_dot(h2, wg_ref[:, f0:f0 + fchunk]) for h2 in h2s]
        ups = [_dot(h2, wu_ref[:, f0:f0 + fchunk]) for h2 in h2s]
        acts = [(gt * _sigmoid(gt) * up).astype(BF16) for gt, up in zip(gts, ups)]
        ys = [y + _dot(act, wd_ref[f0:f0 + fchunk, :]) for y, act in zip(ys, acts)]
    for rows, x1, y in zip(groups, x1s, ys):
        o_ref[:, rows, :] = _layer_norm(alpha * x1 + (1.0 + mod[:, 5:6, :]) * y.reshape(bb, tr, d),
                                        l2g_ref[...], l2b_ref[...], LN_EPS)


def _ffn_chunk(f):
    for n in range(FFN_CHUNK_MAX // MXU_COLS, 0, -1):
        if f % (n * MXU_COLS) == 0:
            return n * MXU_COLS
    return f


def _layer_tail(x, mod, hm, hf, hg, wo, l1g, l1b, wg, wu, wd, l2g, l2b, bb, ts, alpha, hf_transposed):
    b, s, d = x.shape
    f = wg.shape[1]
    row = lambda width: pl.BlockSpec((bb, ts, width), lambda i, j: (i, j, 0))
    vec = lambda a: a.reshape(1, d)
    if hf_transposed:
        assert bb == 1
        hf_spec = pl.BlockSpec((1, hf.shape[1], ts), lambda i, j: (i, 0, j))
    else:
        hf_spec = row(hf.shape[2])
    return pl.pallas_call(
        functools.partial(_tail_kernel, alpha=alpha, fchunk=_ffn_chunk(f), hf_transposed=hf_transposed),
        out_shape=jax.ShapeDtypeStruct((b, s, d), F32),
        grid=(b // bb, s // ts),
        in_specs=[row(d), pl.BlockSpec((bb, 6, d), lambda i, j: (i, 0, 0)),
                  row(hm.shape[2]), hf_spec, row(hg.shape[2]),
                  _resident(wo.shape), _resident((1, d)), _resident((1, d)),
                  _resident(wg.shape), _resident(wu.shape), _resident(wd.shape),
                  _resident((1, d)), _resident((1, d))],
        out_specs=row(d),
        compiler_params=_params("parallel", "parallel"),
        name="layer_tail",
    )(x, mod, hm, hf, hg, wo, vec(l1g), vec(l1b), wg, wu, wd, vec(l2g), vec(l2b))


def _tile(b, s):
    if s >= ROW_TILE:
        return 1, ROW_TILE
    bb = max(1, min(b, ROW_TILE // s))
    while b % bb:
        bb -= 1
    return bb, s


def _layer(x, mod, p, cache, dims, kv_prev=None):
    nh_m, nh_f, ng, hd = dims
    b, s, d = x.shape
    dm, df, dg = nh_m * hd, nh_f * hd, ng * hd
    bb, ts = _tile(b, s)
    alpha = p["alpha"]
    lc = min(s, p["gmlp_ws"].shape[-1])
    gm = (p["gmlp_ws"], p["gmlp_bs_t"], p["gmlp_ln_g"], p["gmlp_ln_b"])
    nrow = 2 * nh_m + nh_f

    if cache is None:
        gate_bias_row = jnp.pad(p["gate_bias"].reshape(1, nrow), ((0, 0), (0, LANE - nrow)))
        m_all, qa, ka, vte, k_t, v_t, hg, gates = _in_projection_prompt(
            x, mod, p["w_in"], gate_bias_row, gm, kv_prev, p["layer"], p["depth"], ts, 4 * dm, df, 2 * dg,
            hd ** -0.5 * LOG2E, ng, hd, lc, nh_f, 2 * nh_m)
    else:
        m_all, fq, fk, fv, fkb, fvb, hg, vn, gates = _in_projection_sample(
            x, mod, p["w_in"], gm, bb, ts, 4 * dm, df, 2 * dg, hd ** -0.5, ng, hd, lc)

    gates_t = jnp.transpose(gates[:, :, :nrow], (0, 2, 1))
    lm = s if cache is not None else min(MLSTM_PROMPT_CHUNK, s)
    gate_rows = _gate_prepass(gates_t, p["gate_bias"], nh_m, nh_f, lm)
    logf = jnp.transpose(gate_rows[:, 2 * nh_m:2 * nh_m + nh_f, :], (0, 2, 1))
    frow = 2 * nh_m + nh_f

    if cache is None:
        c0 = jnp.zeros((b, nh_m, hd, hd), F32)
        n0 = jnp.zeros((b, nh_m, hd, hd), F32)
        m0 = jnp.full((b, 1, nh_m), -jnp.inf, F32)
    else:
        k_c, v_c, logf_c, c_st, n_st, m_st = cache
        c0 = c_st
        n0 = jnp.broadcast_to(n_st[..., None], (b, nh_m, hd, hd))
        m0 = m_st.reshape(b, 1, nh_m)
    gate_cols = jnp.transpose(gate_rows[:, :2 * nh_m, :], (0, 2, 1))
    hm, c1, n1, m1 = _mlstm(m_all, gate_rows, gate_cols, c0, n0, m0, p["mlstm_norm_g"], nh_m, hd, lm)

    if cache is None:
        hf = _fox_prompt(qa, ka, vte, nh_f, hd)
    else:
        hf = _fox_sample(fq, fkb, fvb, k_c, v_c, logf_c, p["layer"], gate_rows, nh_f, hd, frow)

    x_out = _layer_tail(x, mod, hm, hf, hg, p["w_o"], p["ln1_g"], p["ln1_b"], p["w_gate"], p["w_up"],
                        p["w_down"], p["ln2_g"], p["ln2_b"], bb, ts, alpha, cache is None)
    if cache is None:
        new = (k_t, v_t, logf, c1, n1[..., 0], m1.reshape(b, nh_m))
    else:
        new = (fk.reshape(b, s, nh_f, hd), fv.reshape(b, s, nh_f, hd), logf,
               c1, n1[..., 0], m1.reshape(b, nh_m), vn)
    return x_out, new


def kernel(x_prompt, x_sample, cache_fox_k, cache_fox_v, cache_fox_logf, state_mlstm_C, state_mlstm_n, state_mlstm_m, c_prompt, c_sample, w_ada, b_ada, w_in, b_mlstm_i, b_mlstm_f, mlstm_norm_g, b_fox_f, gmlp_ln_g, gmlp_ln_b, gmlp_ws, gmlp_bs, w_o, ln1_g, ln1_b, w_gate, w_up, w_down, ln2_g, ln2_b):
    depth = w_in.shape[0]
    nh_m = b_mlstm_i.shape[1]
    nh_f = b_fox_f.shape[1]
    ng = gmlp_ws.shape[1]
    hd = cache_fox_k.shape[-1]
    dm, df, dg = nh_m * hd, nh_f * hd, ng * hd
    d = x_prompt.shape[-1]
    bp = x_prompt.shape[0]
    alpha = (2 * depth) ** 0.25

    o_mi = 4 * dm
    o_fq = o_mi + 2 * nh_m
    o_ff = o_fq + 3 * df
    o_gu = o_ff + nh_f
    npad = LANE - (2 * nh_m + nh_f)

    bs, p_len = cache_fox_k.shape[1], cache_fox_k.shape[2]
    cache_kt = jnp.transpose(cache_fox_k, (0, 1, 3, 4, 2)).reshape(depth, bs, df, p_len)
    cache_vt = jnp.transpose(cache_fox_v, (0, 1, 3, 4, 2)).reshape(depth, bs, df, p_len)
    cache_ft = jnp.transpose(cache_fox_logf, (0, 1, 3, 2))

    xp, xs = x_prompt, x_sample
    c_all = jnp.concatenate([c_prompt, c_sample], axis=0)
    new_p, new_s = [], []
    kv_prev = None
    for l in range(depth):
        wl = w_in[l]
        w_all = jnp.concatenate(
            [wl[:, :o_mi], wl[:, o_fq:o_ff], wl[:, o_gu:o_gu + 2 * dg], wl[:, o_mi:o_fq], wl[:, o_ff:o_gu],
             jnp.zeros((d, npad), wl.dtype)], axis=1).astype(BF16)
        p = dict(
            alpha=alpha, w_in=w_all, layer=l, depth=depth,
            gate_bias=jnp.concatenate([b_mlstm_i[l], b_mlstm_f[l], b_fox_f[l]]).reshape(-1, 1),
            mlstm_norm_g=mlstm_norm_g[l], gmlp_ln_g=gmlp_ln_g[l], gmlp_ln_b=gmlp_ln_b[l],
            gmlp_ws=gmlp_ws[l], gmlp_bs_t=jnp.transpose(gmlp_bs[l]),
            w_o=w_o[l].astype(BF16), ln1_g=ln1_g[l], ln1_b=ln1_b[l],
            w_gate=w_gate[l].astype(BF16), w_up=w_up[l].astype(BF16), w_down=w_down[l].astype(BF16),
            ln2_g=ln2_g[l], ln2_b=ln2_b[l])
        mod = _modulation(c_all, w_ada, b_ada, l).reshape(c_all.shape[0], 6, d)
        xp, st_p = _layer(xp, mod[:bp], p, None, (nh_m, nh_f, ng, hd), kv_prev)
        kv_prev = st_p[:2]
        xs, st_s = _layer(xs, mod[bp:], p, (cache_kt, cache_vt, cache_ft,
                                            state_mlstm_C[l], state_mlstm_n[l], state_mlstm_m[l]),
                          (nh_m, nh_f, ng, hd))
        new_p.append(st_p)
        new_s.append(st_s)

    def stk(lst, i):
        return jnp.stack([e[i] for e in lst], axis=0)

    def heads_last(kv_t):
        sp = kv_t.shape[-1]
        return jnp.transpose(kv_t.reshape(depth, bp, nh_f, hd, sp), (0, 1, 4, 2, 3))

    return (xp, xs,
            heads_last(kv_prev[0]), heads_last(kv_prev[1]), stk(new_p, 2), stk(new_p, 3), stk(new_p, 4), stk(new_p, 5),
            stk(new_s, 0), stk(new_s, 1), stk(new_s, 2), stk(new_s, 3), stk(new_s, 4), stk(new_s, 5),
            stk(new_s, 6))
```
